```python
import math
import jax, jax.numpy as jnp
from jax import lax
import numpy as np

D_MODEL = 1024
BATCH = 4
SEQ = 4096
DEPTH = 2

MEM_LEN = 256
N_BRANCH = 3
MIX_W = D_MODEL // 2
CONV_K = 3
SB_HEADS = 8
SB_HEAD_DIM = MIX_W // SB_HEADS
SB_BLOCK = 128
RET_HEADS = 4
RET_HEAD_DIM = MIX_W // RET_HEADS
RET_CHUNK = 128
XA_HEADS = 4
XA_HEAD_DIM = D_MODEL // XA_HEADS
D_FF = 4 * D_MODEL
ROPE_BASE = 10000.0
EPS = 1e-6
SPLIT_POINTS = tuple(MIX_W * i for i in range(1, 11))
IN_COLS = 10 * MIX_W + N_BRANCH * D_MODEL

kernel_name = "hybrid_gated_conv_stickbreak_retention_block"


def rmsnorm(x, g):
    xf = x.astype(jnp.float32)
    y = xf * lax.rsqrt(jnp.mean(xf * xf, axis=-1, keepdims=True) + EPS)
    return (y * g.astype(jnp.float32)).astype(x.dtype)


def short_conv_mixer(gate_b, gate_c, h_in, conv_w, conv_b):
    s = h_in.shape[1]
    u = gate_c * h_in
    up = jnp.pad(u, ((0, 0), (CONV_K - 1, 0), (0, 0)))
    conv = conv_b + sum(up[:, j:j + s] * conv_w[j] for j in range(CONV_K))
    return gate_b * conv


def stick_breaking_attention(q, k, v):
    b, s, _ = q.shape
    def heads(t):
        return t.reshape(b, s, SB_HEADS, SB_HEAD_DIM).transpose(0, 2, 1, 3)
    q, k, v = heads(q), heads(k), heads(v)
    scale = SB_HEAD_DIM ** -0.5
    outs = []
    for start in range(0, s, SB_BLOCK):
        end = start + SB_BLOCK
        qb = q[:, :, start:end]
        kb = k[:, :, :end]
        vb = v[:, :, :end]
        z = jnp.einsum('bhtd,bhsd->bhts', qb, kb).astype(jnp.float32) * scale
        t_idx = start + jnp.arange(SB_BLOCK)[:, None]
        s_idx = jnp.arange(end)[None, :]
        causal = s_idx < t_idx
        log_not = jnp.where(causal, jax.nn.log_sigmoid(-z), 0.0)
        between = lax.cumsum(log_not, axis=3, reverse=True) - log_not
        a = jnp.where(causal, jnp.exp(jax.nn.log_sigmoid(z) + between), 0.0)
        outs.append(jnp.einsum('bhts,bhsd->bhtd', a.astype(vb.dtype), vb))
    o = jnp.concatenate(outs, axis=2)
    return o.transpose(0, 2, 1, 3).reshape(b, s, MIX_W)


def rotary(t):
    s, d = t.shape[1], t.shape[3]
    pos = jnp.arange(s, dtype=jnp.float32)
    inv_freq = ROPE_BASE ** (-jnp.arange(0, d, 2, dtype=jnp.float32) / d)
    ang = pos[:, None] * inv_freq[None, :]
    cos = jnp.cos(ang)[None, :, None, :]
    sin = jnp.sin(ang)[None, :, None, :]
    t1, t2 = jnp.split(t, 2, axis=-1)
    return jnp.concatenate([t1 * cos - t2 * sin, t1 * sin + t2 * cos], axis=-1)


def retention(q, k, v, g, gn_gain):
    b, s, _ = q.shape
    nc = s // RET_CHUNK
    f32 = jnp.float32
    def heads(t):
        return t.astype(f32).reshape(b, s, RET_HEADS, RET_HEAD_DIM)
    qh = rotary(heads(q))
    kh = rotary(heads(k)) * RET_HEAD_DIM ** -0.5
    vh = heads(v)
    def chunks(t):
        return t.reshape(b, nc, RET_CHUNK, RET_HEADS, RET_HEAD_DIM).transpose(0, 3, 1, 2, 4)
    qc, kc, vc = chunks(qh), chunks(kh), chunks(vh)
    log_gamma = jnp.log1p(-jnp.exp2(-5.0 - jnp.arange(RET_HEADS, dtype=f32)))
    idx = jnp.arange(RET_CHUNK, dtype=f32)
    rel = idx[:, None] - idx[None, :]
    decay_intra = jnp.where(rel >= 0, jnp.exp(jnp.maximum(rel, 0.0)[None] * log_gamma[:, None, None]), 0.0)
    scores = jnp.einsum('bhnid,bhnjd->bhnij', qc, kc) * decay_intra[None, :, None]
    o_inner = jnp.einsum('bhnij,bhnje->bhnie', scores, vc)
    k_decay = jnp.exp((RET_CHUNK - 1 - idx)[None] * log_gamma[:, None])
    kv = jnp.einsum('bhnjd,bhnje->bhnde', kc * k_decay[None, :, None, :, None], vc)
    chunk_decay = jnp.exp(RET_CHUNK * log_gamma)[None, :, None, None]
    def step(state, kv_n):
        return chunk_decay * state + kv_n, state
    init = jnp.zeros((b, RET_HEADS, RET_HEAD_DIM, RET_HEAD_DIM), f32)
    _, prev = lax.scan(step, init, jnp.moveaxis(kv, 2, 0))
    prev = jnp.moveaxis(prev, 0, 2)
    q_decay = jnp.exp((idx + 1)[None] * log_gamma[:, None])
    o_cross = jnp.einsum('bhnid,bhnde->bhnie', qc, prev) * q_decay[None, :, None, :, None]
    o = o_inner + o_cross
    mu = jnp.mean(o, axis=-1, keepdims=True)
    var = jnp.mean(jnp.square(o - mu), axis=-1, keepdims=True)
    o = (o - mu) * lax.rsqrt(var + EPS)
    o = o.transpose(0, 2, 3, 1, 4).reshape(b, s, MIX_W) * gn_gain.astype(f32)
    return (jax.nn.silu(g.astype(f32)) * o).astype(g.dtype)


def memory_attention(h, mem_n, w_q, w_kv, w_out):
    b, s, _ = h.shape
    m = mem_n.shape[1]
    q = (h @ w_q).reshape(b, s, XA_HEADS, XA_HEAD_DIM)
    k, v = jnp.split(mem_n @ w_kv, 2, axis=-1)
    k = k.reshape(b, m, XA_HEADS, XA_HEAD_DIM)
    v = v.reshape(b, m, XA_HEADS, XA_HEAD_DIM)
    sc = jnp.einsum('bshd,bmhd->bhsm', q, k).astype(jnp.float32) * XA_HEAD_DIM ** -0.5
    p = jax.nn.softmax(sc, axis=-1).astype(v.dtype)
    o = jnp.einsum('bhsm,bmhd->bshd', p, v).reshape(b, s, D_MODEL)
    return o @ w_out


def squared_relu_mlp(h, w_up, w_down):
    return jnp.square(jax.nn.relu(h @ w_up)) @ w_down


def setup_inputs(seed: int = 0) -> dict:
    key = jax.random.key(seed)
    ks = jax.random.split(key, 20)
    f32 = jnp.float32
    def nrm(k, shape, scale):
        return jax.random.normal(k, shape, f32) * scale
    return {
        "x": nrm(ks[0], (BATCH, SEQ, D_MODEL), 1.0),
        "mem": nrm(ks[1], (BATCH, MEM_LEN, D_MODEL), 1.0),
        "norm_mix_g": 1.0 + nrm(ks[2], (DEPTH, D_MODEL), 0.02),
        "w_in": nrm(ks[3], (DEPTH, D_MODEL, IN_COLS), D_MODEL ** -0.5),
        "b_gate": nrm(ks[4], (DEPTH, N_BRANCH, D_MODEL), 0.02),
        "conv_w": nrm(ks[5], (DEPTH, CONV_K, MIX_W), CONV_K ** -0.5),
        "conv_b": nrm(ks[6], (DEPTH, MIX_W), 0.02),
        "ret_norm_g": 1.0 + nrm(ks[7], (DEPTH, MIX_W), 0.02),
        "w_branch": nrm(ks[8], (DEPTH, N_BRANCH, MIX_W, D_MODEL), MIX_W ** -0.5),
        "w_o": nrm(ks[9], (DEPTH, D_MODEL, D_MODEL), D_MODEL ** -0.5),
        "norm_xa_g": 1.0 + nrm(ks[10], (DEPTH, D_MODEL), 0.02),
        "norm_mem_g": 1.0 + nrm(ks[11], (DEPTH, D_MODEL), 0.02),
        "w_xq": nrm(ks[12], (DEPTH, D_MODEL, D_MODEL), D_MODEL ** -0.5),
        "w_xkv": nrm(ks[13], (DEPTH, D_MODEL, 2 * D_MODEL), D_MODEL ** -0.5),
        "w_xo": nrm(ks[14], (DEPTH, D_MODEL, D_MODEL), D_MODEL ** -0.5),
        "norm_mlp_g": 1.0 + nrm(ks[15], (DEPTH, D_MODEL), 0.02),
        "w_up": nrm(ks[16], (DEPTH, D_MODEL, D_FF), D_MODEL ** -0.5),
        "w_down": nrm(ks[17], (DEPTH, D_FF, D_MODEL), D_FF ** -0.5),
        "final_g": 1.0 + nrm(ks[18], (D_MODEL,), 0.02),
    }


def reference(x, mem, norm_mix_g, w_in, b_gate, conv_w, conv_b, ret_norm_g, w_branch, w_o,
              norm_xa_g, norm_mem_g, w_xq, w_xkv, w_xo, norm_mlp_g, w_up, w_down, final_g):
    b, s, _ = x.shape
    for l in range(DEPTH):
        h = rmsnorm(x, norm_mix_g[l])
        proj = h @ w_in[l]
        cb, cc, ch, sq, sk, sv, rq, rk, rv, rg, gate_logits = jnp.split(proj, SPLIT_POINTS, axis=-1)
        conv_out = short_conv_mixer(cb, cc, ch, conv_w[l], conv_b[l])
        sb_out = stick_breaking_attention(sq, sk, sv)
        ret_out = retention(rq, rk, rv, rg, ret_norm_g[l])
        branches = jnp.stack([conv_out, sb_out, ret_out], axis=2)
        up = jnp.einsum('bsnw,nwd->bsnd', branches, w_branch[l])
        gates = jax.nn.sigmoid(gate_logits.reshape(b, s, N_BRANCH, D_MODEL) + b_gate[l])
        merged = jnp.sum(gates * up, axis=2)
        x = x + merged @ w_o[l]
        h = rmsnorm(x, norm_xa_g[l])
        mem_n = rmsnorm(mem, norm_mem_g[l])
        x = x + memory_attention(h, mem_n, w_xq[l], w_xkv[l], w_xo[l])
        h = rmsnorm(x, norm_mlp_g[l])
        x = x + squared_relu_mlp(h, w_up[l], w_down[l])
    return rmsnorm(x, final_g)
```

```python
import functools
import math

import jax
import jax.numpy as jnp
from jax import lax
from jax.experimental import pallas as pl
from jax.experimental.pallas import tpu as pltpu

D_MODEL = 1024
MEM_LEN = 256
N_BRANCH = 3
MIX_W = D_MODEL // 2
CONV_K = 3
SB_HEADS = 8
SB_HEAD_DIM = MIX_W // SB_HEADS
RET_HEADS = 4
RET_HEAD_DIM = MIX_W // RET_HEADS
RET_CHUNK = 128
XA_HEADS = 4
XA_HEAD_DIM = D_MODEL // XA_HEADS
D_FF = 4 * D_MODEL
ROPE_BASE = 10000.0
EPS = 1e-6
IN_COLS = 10 * MIX_W + N_BRANCH * D_MODEL

LANES = 128
VMEM_LIMIT = 48 * 1024 * 1024

F32 = jnp.float32
BF16 = jnp.bfloat16


def _params(semantics):
    return pltpu.CompilerParams(dimension_semantics=semantics, vmem_limit_bytes=VMEM_LIMIT)


def _rmsnorm_f32(xf, g):
    return xf * lax.rsqrt(jnp.mean(xf * xf, axis=-1, keepdims=True) + EPS) * g


def _norm_matmul_kernel(x_ref, g_ref, w_ref, o_ref, h_ref):
    @pl.when(pl.program_id(1) == 0)
    def _():
        h_ref[...] = _rmsnorm_f32(x_ref[...], g_ref[...]).astype(BF16)

    o_ref[...] = jnp.dot(h_ref[...], w_ref[...], preferred_element_type=F32).astype(o_ref.dtype)


def _norm_matmul(x2d, g, w, tm, tn):
    m, d = x2d.shape
    n = w.shape[1]
    return pl.pallas_call(
        _norm_matmul_kernel,
        grid=(m // tm, n // tn),
        in_specs=[
            pl.BlockSpec((tm, d), lambda i, j: (i, 0)),
            pl.BlockSpec((1, d), lambda i, j: (0, 0)),
            pl.BlockSpec((d, tn), lambda i, j: (0, j)),
        ],
        out_specs=pl.BlockSpec((tm, tn), lambda i, j: (i, j)),
        out_shape=jax.ShapeDtypeStruct((m, n), BF16),
        scratch_shapes=[pltpu.VMEM((tm, d), BF16)],
        compiler_params=_params(("parallel", "arbitrary")),
        name="norm_matmul",
    )(x2d, g.reshape(1, d), w)


SB_T = 256


def _sb_tile(qh, kb, vb, upper, carry, acc, causal):
    z = lax.dot_general(qh, kb, (((1,), (1,)), ((), ())), preferred_element_type=F32)
    log_not = -(jnp.maximum(z, 0.0) + jnp.log(1.0 + jnp.exp(-jnp.abs(z))))
    if causal is not None:
        log_not = jnp.where(causal, log_not, 0.0)
    hi = log_not.astype(BF16)
    lo = (log_not - hi.astype(F32)).astype(BF16)
    suffix = (jnp.dot(hi, upper, preferred_element_type=F32)
              + jnp.dot(lo, upper, preferred_element_type=F32))
    a = jnp.exp(z + log_not + suffix + carry)
    if causal is not None:
        a = jnp.where(causal, a, 0.0)
    acc = acc + jnp.dot(a.astype(BF16), vb, preferred_element_type=F32)
    carry = carry + suffix[:, 0:1] + log_not[:, 0:1]
    return carry, acc


def _sb_kernel(q_ref, k_ref, v_ref, o_ref):
    t = SB_T
    qi = pl.program_id(2)
    q2 = q_ref[0]
    lane = lax.broadcasted_iota(jnp.int32, (t, LANES), 1)
    first = lane < SB_HEAD_DIM
    scale = jnp.asarray(SB_HEAD_DIM ** -0.5, BF16)
    zero = jnp.zeros_like(q2)
    qa = jnp.where(first, q2, zero) * scale
    qb = jnp.where(first, zero, q2) * scale
    row = lax.broadcasted_iota(jnp.int32, (t, t), 0)
    col = lax.broadcasted_iota(jnp.int32, (t, t), 1)
    upper = jnp.where(row > col, 1.0, 0.0).astype(BF16)
    causal = col < row

    start = pl.multiple_of(qi * t, t)
    kd = k_ref[0, pl.ds(start, t), :]
    vd = v_ref[0, pl.ds(start, t), :]
    c0 = jnp.zeros((t, 1), F32)
    a0 = jnp.zeros((t, LANES), F32)
    ca, aa = _sb_tile(qa, kd, vd, upper, c0, a0, causal)
    cb, ab = _sb_tile(qb, kd, vd, upper, c0, a0, causal)

    def body(j, state):
        ca, aa, cb, ab = state
        ks = pl.multiple_of((qi - 1 - j) * t, t)
        kb = k_ref[0, pl.ds(ks, t), :]
        vb = v_ref[0, pl.ds(ks, t), :]
        ca, aa = _sb_tile(qa, kb, vb, upper, ca, aa, None)
        cb, ab = _sb_tile(qb, kb, vb, upper, cb, ab, None)
        return ca, aa, cb, ab

    _, aa, _, ab = lax.fori_loop(0, qi, body, (ca, aa, cb, ab))
    o_ref[0] = jnp.where(first, aa, ab).astype(o_ref.dtype)


def _sb_attention(proj, b, s):
    t = SB_T
    qc, kc, vc = (3 * MIX_W) // LANES, (4 * MIX_W) // LANES, (5 * MIX_W) // LANES
    pairs = MIX_W // LANES
    return pl.pallas_call(
        _sb_kernel,
        grid=(b, pairs, s // t),
        in_specs=[
            pl.BlockSpec((1, t, LANES), lambda bi, p, i: (bi, i, qc + p)),
            pl.BlockSpec((1, s, LANES), lambda bi, p, i: (bi, 0, kc + p)),
            pl.BlockSpec((1, s, LANES), lambda bi, p, i: (bi, 0, vc + p)),
        ],
        out_specs=pl.BlockSpec((1, t, LANES), lambda bi, p, i: (bi, i, p)),
        out_shape=jax.ShapeDtypeStruct((b, s, MIX_W), BF16),
        compiler_params=_params(("parallel", "parallel", "arbitrary")),
        name="sb_attention",
    )(proj, proj, proj)


RET_STEP_CHUNKS = 4


def _rotary(tf, cos2, sin2):
    return tf * cos2 + pltpu.roll(tf, RET_HEAD_DIM // 2, 1) * sin2


def _ret_kernel(q_ref, k_ref, v_ref, g_ref, cos_ref, sin_ref, gain_ref, dintra_ref, kdec_ref,
                qdec_ref, cdec_ref, o_ref, state_ref):
    c = RET_CHUNK

    @pl.when(pl.program_id(2) == 0)
    def _():
        state_ref[...] = jnp.zeros_like(state_ref)

    dintra = dintra_ref[0]
    kdec = kdec_ref[0]
    qdec = qdec_ref[0]
    cdec = cdec_ref[0]
    gain = gain_ref[...]
    for n in range(RET_STEP_CHUNKS):
        rows = pl.ds(n * c, c)
        cos2 = cos_ref[rows, :]
        sin2 = sin_ref[rows, :]
        q = _rotary(q_ref[0, rows, :].astype(F32), cos2, sin2)
        k = _rotary(k_ref[0, rows, :].astype(F32), cos2, sin2) * (RET_HEAD_DIM ** -0.5)
        v = v_ref[0, rows, :]
        qb = q.astype(BF16)
        state = state_ref[...]
        scores = lax.dot_general(qb, k.astype(BF16), (((1,), (1,)), ((), ())),
                                 preferred_element_type=F32) * dintra
        o = jnp.dot(scores.astype(BF16), v, preferred_element_type=F32)
        o = o + jnp.dot(qb, state.astype(BF16), preferred_element_type=F32) * qdec
        kv = lax.dot_general((k * kdec).astype(BF16), v, (((0,), (0,)), ((), ())),
                             preferred_element_type=F32)
        state_ref[...] = cdec * state + kv
        mu = jnp.mean(o, axis=-1, keepdims=True)
        oc = o - mu
        var = jnp.mean(oc * oc, axis=-1, keepdims=True)
        on = oc * lax.rsqrt(var + EPS) * gain
        gf = g_ref[0, rows, :].astype(F32)
        o_ref[0, rows, :] = (gf * jax.nn.sigmoid(gf) * on).astype(o_ref.dtype)


def _retention_tables(s):
    pos = jnp.arange(s, dtype=F32)
    inv_freq = ROPE_BASE ** (-jnp.arange(0, RET_HEAD_DIM, 2, dtype=F32) / RET_HEAD_DIM)
    ang = pos[:, None] * inv_freq[None, :]
    cos, sin = jnp.cos(ang), jnp.sin(ang)
    cos2 = jnp.concatenate([cos, cos], axis=-1)
    sin2 = jnp.concatenate([-sin, sin], axis=-1)
    log_gamma = jnp.log1p(-jnp.exp2(-5.0 - jnp.arange(RET_HEADS, dtype=F32)))
    idx = jnp.arange(RET_CHUNK, dtype=F32)
    rel = idx[:, None] - idx[None, :]
    dintra = jnp.where(rel >= 0, jnp.exp(jnp.maximum(rel, 0.0)[None] * log_gamma[:, None, None]), 0.0)
    kdec = jnp.exp((RET_CHUNK - 1 - idx)[None] * log_gamma[:, None])
    qdec = jnp.exp((idx + 1)[None] * log_gamma[:, None])
    cdec = jnp.exp(RET_CHUNK * log_gamma)
    bc = lambda a: jnp.broadcast_to(a[:, :, None], (RET_HEADS, RET_CHUNK, RET_HEAD_DIM))
    cdec = jnp.broadcast_to(cdec[:, None, None], (RET_HEADS, 1, RET_HEAD_DIM))
    return cos2, sin2, dintra, bc(kdec), bc(qdec), cdec


def _retention(proj, gain, b, s):
    ts = RET_STEP_CHUNKS * RET_CHUNK
    c, d = RET_CHUNK, RET_HEAD_DIM
    base = (6 * MIX_W) // d
    cos2, sin2, dintra, kdec, qdec, cdec = _retention_tables(s)
    col = lambda off: pl.BlockSpec((1, ts, d), lambda bi, h, i: (bi, i, base + off + h))
    tab = pl.BlockSpec((ts, d), lambda bi, h, i: (i, 0))
    per_head = lambda r: pl.BlockSpec((1, r, d), lambda bi, h, i: (h, 0, 0))
    return pl.pallas_call(
        _ret_kernel,
        grid=(b, RET_HEADS, s // ts),
        in_specs=[col(0), col(4), col(8), col(12), tab, tab,
                  pl.BlockSpec((1, d), lambda bi, h, i: (0, h)),
                  per_head(c), per_head(c), per_head(c), per_head(1)],
        out_specs=pl.BlockSpec((1, ts, d), lambda bi, h, i: (bi, i, h)),
        out_shape=jax.ShapeDtypeStruct((b, s, MIX_W), BF16),
        scratch_shapes=[pltpu.VMEM((d, d), F32)],
        compiler_params=_params(("parallel", "parallel", "arbitrary")),
        name="retention",
    )(proj, proj, proj, proj, cos2, sin2, gain.reshape(1, MIX_W), dintra, kdec, qdec, cdec)


MERGE_TM = 512
HALO = 8


def _merge_kernel(x_ref, cb_ref, cc_ref, ch_ref, ccp_ref, chp_ref, gl0_ref, gl1_ref, gl2_ref,
                  sb_ref, ret_ref, convw_ref, convb_ref, wbr_ref, bg_ref, wo_ref, o_ref):
    tm = MERGE_TM
    gl_refs = (gl0_ref, gl1_ref, gl2_ref)
    u = cc_ref[0].astype(F32) * ch_ref[0].astype(F32)
    prev = ccp_ref[0].astype(F32) * chp_ref[0].astype(F32)
    prev = jnp.where(pl.program_id(1) == 0, 0.0, prev)
    row = lax.broadcasted_iota(jnp.int32, (tm, MIX_W), 0)
    u1 = jnp.where(row == 0, prev[HALO - 1:HALO, :], pltpu.roll(u, 1, 0))
    u2 = jnp.where(row == 0, prev[HALO - 2:HALO - 1, :],
                   jnp.where(row == 1, prev[HALO - 1:HALO, :], pltpu.roll(u, 2, 0)))
    conv = convb_ref[...] + u2 * convw_ref[0:1, :] + u1 * convw_ref[1:2, :] + u * convw_ref[2:3, :]
    conv_out = cb_ref[0].astype(F32) * conv
    branches = (conv_out.astype(BF16), sb_ref[0], ret_ref[0])
    merged = jnp.zeros((tm, D_MODEL), F32)
    for n in range(N_BRANCH):
        up = jnp.dot(branches[n], wbr_ref[n], preferred_element_type=F32)
        logits = gl_refs[n][0].astype(F32) + bg_ref[n:n + 1, :]
        merged = merged + jax.nn.sigmoid(logits) * up
    o_ref[0] = x_ref[0] + jnp.dot(merged.astype(BF16), wo_ref[...], preferred_element_type=F32)


def _merge(x, proj, sb_out, ret_out, conv_w, conv_b, w_branch, b_gate, w_o):
    b, s, d = x.shape
    tm = MERGE_TM
    w = MIX_W
    gate_blk = (10 * MIX_W) // d
    tok = lambda width, cblk: pl.BlockSpec((1, tm, width), lambda bi, i: (bi, i, cblk))
    halo = lambda cblk: pl.BlockSpec(
        (1, HALO, w), lambda bi, i: (bi, jnp.maximum(i * (tm // HALO) - 1, 0), cblk))
    full = lambda shape: pl.BlockSpec(shape, lambda bi, i: (0,) * len(shape))
    return pl.pallas_call(
        _merge_kernel,
        grid=(b, s // tm),
        in_specs=[tok(d, 0), tok(w, 0), tok(w, 1), tok(w, 2), halo(1), halo(2),
                  tok(d, gate_blk), tok(d, gate_blk + 1), tok(d, gate_blk + 2), tok(w, 0), tok(w, 0),
                  full((CONV_K, w)), full((1, w)), full((N_BRANCH, w, d)), full((N_BRANCH, d)),
                  full((d, d))],
        out_specs=tok(d, 0),
        out_shape=jax.ShapeDtypeStruct((b, s, d), F32),
        compiler_params=_params(("parallel", "parallel")),
        name="merge",
    )(x, proj, proj, proj, proj, proj, proj, proj, proj, sb_out, ret_out, conv_w,
      conv_b.reshape(1, w),
      w_branch, b_gate, w_o)


XA_TM = 512


def _xattn_kernel(x_ref, g_ref, wq_ref, kv_ref, wo_ref, o_ref):
    x = x_ref[0]
    h = _rmsnorm_f32(x, g_ref[...]).astype(BF16)
    q = jnp.dot(h, wq_ref[...], preferred_element_type=F32)
    q = (q * (XA_HEAD_DIM ** -0.5)).astype(BF16)
    outs = []
    for hd in range(XA_HEADS):
        lo = hd * XA_HEAD_DIM
        qh = q[:, lo:lo + XA_HEAD_DIM]
        kh = kv_ref[0, :, lo:lo + XA_HEAD_DIM]
        vh = kv_ref[0, :, D_MODEL + lo:D_MODEL + lo + XA_HEAD_DIM]
        sc = lax.dot_general(qh, kh, (((1,), (1,)), ((), ())), preferred_element_type=F32)
        e = jnp.exp(sc - jnp.max(sc, axis=-1, keepdims=True))
        p = e / jnp.sum(e, axis=-1, keepdims=True)
        outs.append(jnp.dot(p.astype(BF16), vh, preferred_element_type=F32).astype(BF16))
    o = jnp.concatenate(outs, axis=-1)
    o_ref[0] = x + jnp.dot(o, wo_ref[...], preferred_element_type=F32)


def _xattn(x, g, w_q, kv, w_o):
    b, s, d = x.shape
    tm = XA_TM
    full = lambda shape: pl.BlockSpec(shape, lambda bi, i: (0,) * len(shape))
    return pl.pallas_call(
        _xattn_kernel,
        grid=(b, s // tm),
        in_specs=[pl.BlockSpec((1, tm, d), lambda bi, i: (bi, i, 0)), full((1, d)), full((d, d)),
                  pl.BlockSpec((1, MEM_LEN, 2 * d), lambda bi, i: (bi, 0, 0)), full((d, d))],
        out_specs=pl.BlockSpec((1, tm, d), lambda bi, i: (bi, i, 0)),
        out_shape=jax.ShapeDtypeStruct((b, s, d), F32),
        compiler_params=_params(("parallel", "parallel")),
        name="xattn",
    )(x, g.reshape(1, d), w_q, kv, w_o)


MLP_TM = 1024
MLP_TF = 1024


def _mlp_kernel(x_ref, g_ref, wu_ref, wd_ref, fg_ref, o_ref, h_ref, acc_ref, *, final_norm):
    j = pl.program_id(1)

    @pl.when(j == 0)
    def _():
        h_ref[...] = _rmsnorm_f32(x_ref[...], g_ref[...]).astype(BF16)
        acc_ref[...] = x_ref[...]

    up = jnp.dot(h_ref[...], wu_ref[...], preferred_element_type=F32)
    act = jnp.square(jnp.maximum(up, 0.0)).astype(BF16)
    acc_ref[...] += jnp.dot(act, wd_ref[...], preferred_element_type=F32)

    @pl.when(j == pl.num_programs(1) - 1)
    def _():
        y = acc_ref[...]
        o_ref[...] = _rmsnorm_f32(y, fg_ref[...]) if final_norm else y


def _mlp(x2d, g, w_up, w_down, final_g, final_norm):
    m, d = x2d.shape
    f = w_up.shape[1]
    tm, tf = MLP_TM, MLP_TF
    return pl.pallas_call(
        functools.partial(_mlp_kernel, final_norm=final_norm),
        grid=(m // tm, f // tf),
        in_specs=[pl.BlockSpec((tm, d), lambda i, j: (i, 0)),
                  pl.BlockSpec((1, d), lambda i, j: (0, 0)),
                  pl.BlockSpec((d, tf), lambda i, j: (0, j)),
                  pl.BlockSpec((tf, d), lambda i, j: (j, 0)),
                  pl.BlockSpec((1, d), lambda i, j: (0, 0))],
        out_specs=pl.BlockSpec((tm, d), lambda i, j: (i, 0)),
        out_shape=jax.ShapeDtypeStruct((m, d), F32),
        scratch_shapes=[pltpu.VMEM((tm, d), BF16), pltpu.VMEM((tm, d), F32)],
        compiler_params=_params(("parallel", "arbitrary")),
        name="mlp",
    )(x2d, g.reshape(1, d), w_up, w_down, final_g.reshape(1, d))


def kernel(x, mem, norm_mix_g, w_in, b_gate, conv_w, conv_b, ret_norm_g, w_branch, w_o,
           norm_xa_g, norm_mem_g, w_xq, w_xkv, w_xo, norm_mlp_g, w_up, w_down, final_g):
    b, s, d = x.shape
    depth = w_in.shape[0]
    assert d == D_MODEL and s % SB_T == 0 and s % MERGE_TM == 0 and (b * s) % MLP_TM == 0
    mem2d = mem.reshape(b * MEM_LEN, d)
    for l in range(depth):
        proj = _norm_matmul(x.reshape(b * s, d), norm_mix_g[l], w_in[l].astype(BF16), 1024, 1024)
        proj = proj.reshape(b, s, IN_COLS)
        sb_out = _sb_attention(proj, b, s)
        ret_out = _retention(proj, ret_norm_g[l], b, s)
        x = _merge(x, proj, sb_out, ret_out, conv_w[l], conv_b[l], w_branch[l].astype(BF16),
                   b_gate[l], w_o[l].astype(BF16))
        kv = _norm_matmul(mem2d, norm_mem_g[l], w_xkv[l].astype(BF16), b * MEM_LEN, 1024)
        x = _xattn(x, norm_xa_g[l], w_xq[l].astype(BF16), kv.reshape(b, MEM_LEN, 2 * d),
                   w_xo[l].astype(BF16))
        x = _mlp(x.reshape(b * s, d), norm_mlp_g[l], w_up[l].astype(BF16), w_down[l].astype(BF16),
                 final_g, l == depth - 1).reshape(b, s, d)
    return x
```

```python
import functools
import math

import jax
import jax.numpy as jnp
from jax import lax
from jax.experimental import pallas as pl
from jax.experimental.pallas import tpu as pltpu

D_MODEL = 1024
MEM_LEN = 256
N_BRANCH = 3
MIX_W = D_MODEL // 2
CONV_K = 3
SB_HEADS = 8
SB_HEAD_DIM = MIX_W // SB_HEADS
RET_HEADS = 4
RET_HEAD_DIM = MIX_W // RET_HEADS
RET_CHUNK = 128
XA_HEADS = 4
XA_HEAD_DIM = D_MODEL // XA_HEADS
D_FF = 4 * D_MODEL
ROPE_BASE = 10000.0
EPS = 1e-6
IN_COLS = 10 * MIX_W + N_BRANCH * D_MODEL

LANES = 128
VMEM_LIMIT = 48 * 1024 * 1024

F32 = jnp.float32
BF16 = jnp.bfloat16


def _params(semantics):
    return pltpu.CompilerParams(dimension_semantics=semantics, vmem_limit_bytes=VMEM_LIMIT)


def _rmsnorm_f32(xf, g):
    return xf * lax.rsqrt(jnp.mean(xf * xf, axis=-1, keepdims=True) + EPS) * g


def _norm_matmul_kernel(x_ref, g_ref, w_ref, o_ref, h_ref):
    @pl.when(pl.program_id(1) == 0)
    def _():
        h_ref[...] = _rmsnorm_f32(x_ref[...], g_ref[...]).astype(BF16)

    o_ref[...] = jnp.dot(h_ref[...], w_ref[...], preferred_element_type=F32).astype(o_ref.dtype)


def _norm_matmul(x2d, g, w, tm, tn):
    m, d = x2d.shape
    n = w.shape[1]
    return pl.pallas_call(
        _norm_matmul_kernel,
        grid=(m // tm, n // tn),
        in_specs=[
            pl.BlockSpec((tm, d), lambda i, j: (i, 0)),
            pl.BlockSpec((1, d), lambda i, j: (0, 0)),
            pl.BlockSpec((d, tn), lambda i, j: (0, j)),
        ],
        out_specs=pl.BlockSpec((tm, tn), lambda i, j: (i, j)),
        out_shape=jax.ShapeDtypeStruct((m, n), BF16),
        scratch_shapes=[pltpu.VMEM((tm, d), BF16)],
        compiler_params=_params(("parallel", "arbitrary")),
        name="norm_matmul",
    )(x2d, g.reshape(1, d), w)


SB_T = 256
SB_BATCH = 4
LOG2E = math.log2(math.e)


def _sb_scores(qs, k_ref, ks, z_ref):
    for i, q in enumerate(qs):
        kb = k_ref[i // 2, pl.ds(ks, SB_T), :]
        z = lax.dot_general(q, kb, (((1,), (1,)), ((), ())), preferred_element_type=F32)
        z_ref[i] = z * LOG2E


def _sb_values(v_ref, ks, a_ref, acc_ref):
    for i in range(a_ref.shape[0]):
        vb = v_ref[i // 2, pl.ds(ks, SB_T), :]
        acc_ref[i] += jnp.dot(a_ref[i], vb, preferred_element_type=F32)


def _sb_stage1(z2, upper, causal):
    neg_abs = pltpu.bitcast(pltpu.bitcast(z2, jnp.uint32) | jnp.uint32(0x80000000), F32)
    sp = jnp.maximum(z2, 0.0) + jnp.log2(1.0 + jnp.exp2(neg_abs))
    if causal is not None:
        sp = jnp.where(causal, sp, 0.0)
    sp_lo = sp.astype(BF16)
    suffix = jnp.dot(sp_lo, upper, preferred_element_type=F32)
    return z2 - sp, suffix, suffix[:, 0:1] + sp_lo[:, 0:1].astype(F32)


def _sb_step(qs, k_ref, v_ref, ks, upper, z_ref, a_ref, c_ref, acc_ref, causal):
    t = SB_T
    ks_next = pl.multiple_of(jnp.maximum(ks - t, 0), t)
    stage1 = []
    for i, q in enumerate(qs):
        z2 = z_ref[i]
        _sb_scores(qs[i:i + 1], k_ref.at[i // 2:i // 2 + 1], ks_next, z_ref.at[i:i + 1])
        if causal is None:
            _sb_values(v_ref.at[i // 2:i // 2 + 1], pl.multiple_of(ks + t, t), a_ref.at[i:i + 1],
                       acc_ref.at[i:i + 1])
        stage1.append(_sb_stage1(z2, upper, causal))
    for i, (log_beta, suffix, total) in enumerate(stage1):
        carry = c_ref[i]
        a = jnp.exp2(log_beta - suffix - carry)
        if causal is not None:
            a = jnp.where(causal, a, 0.0)
        a_ref[i] = a.astype(BF16)
        c_ref[i] = carry + total


def _sb_kernel(q_ref, k_ref, v_ref, o_ref, z_ref, a_ref, c_ref, acc_ref):
    t = SB_T
    nb = q_ref.shape[0]
    qi = pl.program_id(1)
    lane = lax.broadcasted_iota(jnp.int32, (t, LANES), 1)
    first = lane < SB_HEAD_DIM
    scale = jnp.asarray(SB_HEAD_DIM ** -0.5, BF16)
    qs = []
    for bi in range(nb):
        q2 = q_ref[bi] * scale
        zero = jnp.zeros_like(q2)
        qs += [jnp.where(first, q2, zero), jnp.where(first, zero, q2)]
    row = lax.broadcasted_iota(jnp.int32, (t, t), 0)
    col = lax.broadcasted_iota(jnp.int32, (t, t), 1)
    upper = jnp.where(row > col, 1.0, 0.0).astype(BF16)
    causal = col < row

    c_ref[...] = jnp.zeros_like(c_ref)
    acc_ref[...] = jnp.zeros_like(acc_ref)
    diag = pl.multiple_of(qi * t, t)
    _sb_scores(qs, k_ref, diag, z_ref)
    _sb_step(qs, k_ref, v_ref, diag, upper, z_ref, a_ref, c_ref, acc_ref, causal)

    @pl.loop(0, qi)
    def _(j):
        ks = pl.multiple_of((qi - 1 - j) * t, t)
        _sb_step(qs, k_ref, v_ref, ks, upper, z_ref, a_ref, c_ref, acc_ref, None)

    _sb_values(v_ref, 0, a_ref, acc_ref)
    for bi in range(nb):
        o_ref[bi] = jnp.where(first, acc_ref[2 * bi], acc_ref[2 * bi + 1]).astype(o_ref.dtype)


def _sb_attention(proj, b, s):
    t, nb = SB_T, SB_BATCH
    ns = 2 * nb
    qc, kc, vc = (3 * MIX_W) // LANES, (4 * MIX_W) // LANES, (5 * MIX_W) // LANES
    pairs = MIX_W // LANES
    return pl.pallas_call(
        _sb_kernel,
        grid=(b // nb * pairs, s // t),
        in_specs=[
            pl.BlockSpec((nb, t, LANES), lambda g, i: (g // pairs, i, qc + g % pairs)),
            pl.BlockSpec((nb, s, LANES), lambda g, i: (g // pairs, 0, kc + g % pairs)),
            pl.BlockSpec((nb, s, LANES), lambda g, i: (g // pairs, 0, vc + g % pairs)),
        ],
        out_specs=pl.BlockSpec((nb, t, LANES), lambda g, i: (g // pairs, i, g % pairs)),
        out_shape=jax.ShapeDtypeStruct((b, s, MIX_W), BF16),
        scratch_shapes=[pltpu.VMEM((ns, t, t), F32), pltpu.VMEM((ns, t, t), BF16),
                        pltpu.VMEM((ns, t, 1), F32), pltpu.VMEM((ns, t, LANES), F32)],
        compiler_params=_params(("parallel", "arbitrary")),
        name="sb_attention",
    )(proj, proj, proj)


RET_STEP_CHUNKS = 4


def _rotary(tf, cos2, sin2):
    return tf * cos2 + pltpu.roll(tf, RET_HEAD_DIM // 2, 1) * sin2


def _ret_kernel(q_ref, k_ref, v_ref, g_ref, cos_ref, sin_ref, gain_ref, dintra_ref, kdec_ref,
                qdec_ref, cdec_ref, o_ref, state_ref):
    c, d = RET_CHUNK, RET_HEAD_DIM

    @pl.when(pl.program_id(1) == 0)
    def _():
        state_ref[...] = jnp.zeros_like(state_ref)

    units = [(h, n) for h in range(RET_HEADS) for n in range(RET_STEP_CHUNKS)]
    blk = lambda ref, h, n: ref[0, pl.ds(n * c, c), pl.ds(h * d, d)]
    qb, scores, kv = {}, {}, {}
    for h, n in units:
        cos2 = cos_ref[pl.ds(n * c, c), :]
        sin2 = sin_ref[pl.ds(n * c, c), :]
        q = _rotary(blk(q_ref, h, n).astype(F32), cos2, sin2)
        k = _rotary(blk(k_ref, h, n).astype(F32), cos2, sin2) * (d ** -0.5)
        qb[h, n] = q.astype(BF16)
        scores[h, n] = lax.dot_general(qb[h, n], k.astype(BF16), (((1,), (1,)), ((), ())),
                                       preferred_element_type=F32)
        kv[h, n] = lax.dot_general((k * kdec_ref[h]).astype(BF16), blk(v_ref, h, n),
                                   (((0,), (0,)), ((), ())), preferred_element_type=F32)
    inner = {}
    for h, n in units:
        p = (scores[h, n] * dintra_ref[h]).astype(BF16)
        inner[h, n] = jnp.dot(p, blk(v_ref, h, n), preferred_element_type=F32)
    cross = {}
    for h in range(RET_HEADS):
        state = state_ref[h]
        for n in range(RET_STEP_CHUNKS):
            cross[h, n] = jnp.dot(qb[h, n], state.astype(BF16), preferred_element_type=F32)
            state = cdec_ref[h] * state + kv[h, n]
        state_ref[h] = state
    for h, n in units:
        o = inner[h, n] + cross[h, n] * qdec_ref[h]
        mu = jnp.mean(o, axis=-1, keepdims=True)
        oc = o - mu
        var = jnp.mean(oc * oc, axis=-1, keepdims=True)
        on = oc * lax.rsqrt(var + EPS) * gain_ref[:, pl.ds(h * d, d)]
        gf = blk(g_ref, h, n).astype(F32)
        o_ref[0, pl.ds(n * c, c), pl.ds(h * d, d)] = (gf * jax.nn.sigmoid(gf) * on).astype(o_ref.dtype)


def _retention_tables(s):
    pos = jnp.arange(s, dtype=F32)
    inv_freq = ROPE_BASE ** (-jnp.arange(0, RET_HEAD_DIM, 2, dtype=F32) / RET_HEAD_DIM)
    ang = pos[:, None] * inv_freq[None, :]
    cos, sin = jnp.cos(ang), jnp.sin(ang)
    cos2 = jnp.concatenate([cos, cos], axis=-1)
    sin2 = jnp.concatenate([-sin, sin], axis=-1)
    log_gamma = jnp.log1p(-jnp.exp2(-5.0 - jnp.arange(RET_HEADS, dtype=F32)))
    idx = jnp.arange(RET_CHUNK, dtype=F32)
    rel = idx[:, None] - idx[None, :]
    dintra = jnp.where(rel >= 0, jnp.exp(jnp.maximum(rel, 0.0)[None] * log_gamma[:, None, None]), 0.0)
    kdec = jnp.exp((RET_CHUNK - 1 - idx)[None] * log_gamma[:, None])
    qdec = jnp.exp((idx + 1)[None] * log_gamma[:, None])
    cdec = jnp.exp(RET_CHUNK * log_gamma)
    bc = lambda a: jnp.broadcast_to(a[:, :, None], (RET_HEADS, RET_CHUNK, RET_HEAD_DIM))
    cdec = jnp.broadcast_to(cdec[:, None, None], (RET_HEADS, 1, RET_HEAD_DIM))
    return cos2, sin2, dintra, bc(kdec), bc(qdec), cdec


def _retention(proj, gain, b, s):
    ts = RET_STEP_CHUNKS * RET_CHUNK
    c, d = RET_CHUNK, RET_HEAD_DIM
    base = (6 * MIX_W) // MIX_W
    cos2, sin2, dintra, kdec, qdec, cdec = _retention_tables(s)
    col = lambda off: pl.BlockSpec((1, ts, MIX_W), lambda bi, i: (bi, i, base + off))
    tab = pl.BlockSpec((ts, d), lambda bi, i: (i, 0))
    full = lambda shape: pl.BlockSpec(shape, lambda bi, i: (0,) * len(shape))
    return pl.pallas_call(
        _ret_kernel,
        grid=(b, s // ts),
        in_specs=[col(0), col(1), col(2), col(3), tab, tab, full((1, MIX_W)),
                  full((RET_HEADS, c, c)), full((RET_HEADS, c, d)), full((RET_HEADS, c, d)),
                  full((RET_HEADS, 1, d))],
        out_specs=pl.BlockSpec((1, ts, MIX_W), lambda bi, i: (bi, i, 0)),
        out_shape=jax.ShapeDtypeStruct((b, s, MIX_W), BF16),
        scratch_shapes=[pltpu.VMEM((RET_HEADS, d, d), F32)],
        compiler_params=_params(("parallel", "arbitrary")),
        name="retention",
    )(proj, proj, proj, proj, cos2, sin2, gain.reshape(1, MIX_W), dintra, kdec, qdec, cdec)


MERGE_TM = 512
HALO = 8


def _merge_kernel(x_ref, cb_ref, cc_ref, ch_ref, ccp_ref, chp_ref, gl0_ref, gl1_ref, gl2_ref,
                  sb_ref, ret_ref, convw_ref, convb_ref, wbr_ref, bg_ref, wo_ref, o_ref):
    tm = MERGE_TM
    gl_refs = (gl0_ref, gl1_ref, gl2_ref)
    u = cc_ref[0].astype(F32) * ch_ref[0].astype(F32)
    prev = ccp_ref[0].astype(F32) * chp_ref[0].astype(F32)
    prev = jnp.where(pl.program_id(1) == 0, 0.0, prev)
    row = lax.broadcasted_iota(jnp.int32, (tm, MIX_W), 0)
    u1 = jnp.where(row == 0, prev[HALO - 1:HALO, :], pltpu.roll(u, 1, 0))
    u2 = jnp.where(row == 0, prev[HALO - 2:HALO - 1, :],
                   jnp.where(row == 1, prev[HALO - 1:HALO, :], pltpu.roll(u, 2, 0)))
    conv = convb_ref[...] + u2 * convw_ref[0:1, :] + u1 * convw_ref[1:2, :] + u * convw_ref[2:3, :]
    conv_out = cb_ref[0].astype(F32) * conv
    branches = (conv_out.astype(BF16), sb_ref[0], ret_ref[0])
    merged = jnp.zeros((tm, D_MODEL), F32)
    for n in range(N_BRANCH):
        up = jnp.dot(branches[n], wbr_ref[n], preferred_element_type=F32)
        logits = gl_refs[n][0].astype(F32) + bg_ref[n:n + 1, :]
        merged = merged + jax.nn.sigmoid(logits) * up
    o_ref[0] = x_ref[0] + jnp.dot(merged.astype(BF16), wo_ref[...], preferred_element_type=F32)


def _merge(x, proj, sb_out, ret_out, conv_w, conv_b, w_branch, b_gate, w_o):
    b, s, d = x.shape
    tm = MERGE_TM
    w = MIX_W
    gate_blk = (10 * MIX_W) // d
    tok = lambda width, cblk: pl.BlockSpec((1, tm, width), lambda bi, i: (bi, i, cblk))
    halo = lambda cblk: pl.BlockSpec(
        (1, HALO, w), lambda bi, i: (bi, jnp.maximum(i * (tm // HALO) - 1, 0), cblk))
    full = lambda shape: pl.BlockSpec(shape, lambda bi, i: (0,) * len(shape))
    return pl.pallas_call(
        _merge_kernel,
        grid=(b, s // tm),
        in_specs=[tok(d, 0), tok(w, 0), tok(w, 1), tok(w, 2), halo(1), halo(2),
                  tok(d, gate_blk), tok(d, gate_blk + 1), tok(d, gate_blk + 2), tok(w, 0), tok(w, 0),
                  full((CONV_K, w)), full((1, w)), full((N_BRANCH, w, d)), full((N_BRANCH, d)),
                  full((d, d))],
        out_specs=tok(d, 0),
        out_shape=jax.ShapeDtypeStruct((b, s, d), F32),
        compiler_params=_params(("parallel", "parallel")),
        name="merge",
    )(x, proj, proj, proj, proj, proj, proj, proj, proj, sb_out, ret_out, conv_w,
      conv_b.reshape(1, w),
      w_branch, b_gate, w_o)


XA_TM = 512


def _xattn_kernel(x_ref, g_ref, wq_ref, kv_ref, wo_ref, o_ref):
    x = x_ref[0]
    h = _rmsnorm_f32(x, g_ref[...]).astype(BF16)
    q = jnp.dot(h, wq_ref[...], preferred_element_type=F32)
    q = (q * (XA_HEAD_DIM ** -0.5)).astype(BF16)
    outs = []
    for hd in range(XA_HEADS):
        lo = hd * XA_HEAD_DIM
        qh = q[:, lo:lo + XA_HEAD_DIM]
        kh = kv_ref[0, :, lo:lo + XA_HEAD_DIM]
        vh = kv_ref[0, :, D_MODEL + lo:D_MODEL + lo + XA_HEAD_DIM]
        sc = lax.dot_general(qh, kh, (((1,), (1,)), ((), ())), preferred_element_type=F32)
        e = jnp.exp(sc - jnp.max(sc, axis=-1, keepdims=True))
        p = e / jnp.sum(e, axis=-1, keepdims=True)
        outs.append(jnp.dot(p.astype(BF16), vh, preferred_element_type=F32).astype(BF16))
    o = jnp.concatenate(outs, axis=-1)
    o_ref[0] = x + jnp.dot(o, wo_ref[...], preferred_element_type=F32)


def _xattn(x, g, w_q, kv, w_o):
    b, s, d = x.shape
    tm = XA_TM
    full = lambda shape: pl.BlockSpec(shape, lambda bi, i: (0,) * len(shape))
    return pl.pallas_call(
        _xattn_kernel,
        grid=(b, s // tm),
        in_specs=[pl.BlockSpec((1, tm, d), lambda bi, i: (bi, i, 0)), full((1, d)), full((d, d)),
                  pl.BlockSpec((1, MEM_LEN, 2 * d), lambda bi, i: (bi, 0, 0)), full((d, d))],
        out_specs=pl.BlockSpec((1, tm, d), lambda bi, i: (bi, i, 0)),
        out_shape=jax.ShapeDtypeStruct((b, s, d), F32),
        compiler_params=_params(("parallel", "parallel")),
        name="xattn",
    )(x, g.reshape(1, d), w_q, kv, w_o)


MLP_TM = 1024
MLP_TF = 1024


def _mlp_kernel(x_ref, g_ref, wu_ref, wd_ref, fg_ref, o_ref, h_ref, acc_ref, *, final_norm):
    j = pl.program_id(1)

    @pl.when(j == 0)
    def _():
        h_ref[...] = _rmsnorm_f32(x_ref[...], g_ref[...]).astype(BF16)
        acc_ref[...] = x_ref[...]

    up = jnp.dot(h_ref[...], wu_ref[...], preferred_element_type=F32)
    act = jnp.square(jnp.maximum(up, 0.0)).astype(BF16)
    acc_ref[...] += jnp.dot(act, wd_ref[...], preferred_element_type=F32)

    @pl.when(j == pl.num_programs(1) - 1)
    def _():
        y = acc_ref[...]
        o_ref[...] = _rmsnorm_f32(y, fg_ref[...]) if final_norm else y


def _mlp(x2d, g, w_up, w_down, final_g, final_norm):
    m, d = x2d.shape
    f = w_up.shape[1]
    tm, tf = MLP_TM, MLP_TF
    return pl.pallas_call(
        functools.partial(_mlp_kernel, final_norm=final_norm),
        grid=(m // tm, f // tf),
        in_specs=[pl.BlockSpec((tm, d), lambda i, j: (i, 0)),
                  pl.BlockSpec((1, d), lambda i, j: (0, 0)),
                  pl.BlockSpec((d, tf), lambda i, j: (0, j)),
                  pl.BlockSpec((tf, d), lambda i, j: (j, 0)),
                  pl.BlockSpec((1, d), lambda i, j: (0, 0))],
        out_specs=pl.BlockSpec((tm, d), lambda i, j: (i, 0)),
        out_shape=jax.ShapeDtypeStruct((m, d), F32),
        scratch_shapes=[pltpu.VMEM((tm, d), BF16), pltpu.VMEM((tm, d), F32)],
        compiler_params=_params(("parallel", "arbitrary")),
        name="mlp",
    )(x2d, g.reshape(1, d), w_up, w_down, final_g.reshape(1, d))


def kernel(x, mem, norm_mix_g, w_in, b_gate, conv_w, conv_b, ret_norm_g, w_branch, w_o,
           norm_xa_g, norm_mem_g, w_xq, w_xkv, w_xo, norm_mlp_g, w_up, w_down, final_g):
    b, s, d = x.shape
    depth = w_in.shape[0]
    assert d == D_MODEL and s % SB_T == 0 and s % MERGE_TM == 0 and (b * s) % MLP_TM == 0
    assert b % SB_BATCH == 0
    mem2d = mem.reshape(b * MEM_LEN, d)
    for l in range(depth):
        proj = _norm_matmul(x.reshape(b * s, d), norm_mix_g[l], w_in[l].astype(BF16), 1024, 1024)
        proj = proj.reshape(b, s, IN_COLS)
        sb_out = _sb_attention(proj, b, s)
        ret_out = _retention(proj, ret_norm_g[l], b, s)
        x = _merge(x, proj, sb_out, ret_out, conv_w[l], conv_b[l], w_branch[l].astype(BF16),
                   b_gate[l], w_o[l].astype(BF16))
        kv = _norm_matmul(mem2d, norm_mem_g[l], w_xkv[l].astype(BF16), b * MEM_LEN, 1024)
        x = _xattn(x, norm_xa_g[l], w_xq[l].astype(BF16), kv.reshape(b, MEM_LEN, 2 * d),
                   w_xo[l].astype(BF16))
        x = _mlp(x.reshape(b * s, d), norm_mlp_g[l], w_up[l].astype(BF16), w_down[l].astype(BF16),
                 final_g, l == depth - 1).reshape(b, s, d)
    return x
```

```python
import functools
import math

import jax
import jax.numpy as jnp
from jax import lax
from jax.experimental import pallas as pl
from jax.experimental.pallas import tpu as pltpu

D_MODEL = 1024
MEM_LEN = 256
N_BRANCH = 3
MIX_W = D_MODEL // 2
CONV_K = 3
SB_HEADS = 8
SB_HEAD_DIM = MIX_W // SB_HEADS
RET_HEADS = 4
RET_HEAD_DIM = MIX_W // RET_HEADS
RET_CHUNK = 128
XA_HEADS = 4
XA_HEAD_DIM = D_MODEL // XA_HEADS
D_FF = 4 * D_MODEL
ROPE_BASE = 10000.0
EPS = 1e-6
IN_COLS = 10 * MIX_W + N_BRANCH * D_MODEL

LANES = 128
ROW_SPLIT = 2
VMEM_LIMIT = 48 * 1024 * 1024

F32 = jnp.float32
BF16 = jnp.bfloat16


def _params(semantics):
    return pltpu.CompilerParams(dimension_semantics=semantics, vmem_limit_bytes=VMEM_LIMIT)


def _rmsnorm_f32(xf, g):
    return xf * lax.rsqrt(jnp.mean(xf * xf, axis=-1, keepdims=True) + EPS) * g


def _norm_matmul_kernel(x_ref, g_ref, w_ref, o_ref, h_ref):
    @pl.when(pl.program_id(1) == 0)
    def _():
        h_ref[...] = _rmsnorm_f32(x_ref[...], g_ref[...]).astype(BF16)

    o_ref[...] = jnp.dot(h_ref[...], w_ref[...], preferred_element_type=F32).astype(o_ref.dtype)


def _norm_matmul(x2d, g, w, tm, tn):
    m, d = x2d.shape
    n = w.shape[1]
    return pl.pallas_call(
        _norm_matmul_kernel,
        grid=(m // tm, n // tn),
        in_specs=[
            pl.BlockSpec((tm, d), lambda i, j: (i, 0)),
            pl.BlockSpec((1, d), lambda i, j: (0, 0)),
            pl.BlockSpec((d, tn), lambda i, j: (0, j)),
        ],
        out_specs=pl.BlockSpec((tm, tn), lambda i, j: (i, j)),
        out_shape=jax.ShapeDtypeStruct((m, n), BF16),
        scratch_shapes=[pltpu.VMEM((tm, d), BF16)],
        compiler_params=_params(("parallel", "arbitrary")),
        name="norm_matmul",
    )(x2d, g.reshape(1, d), w)


SB_T = 256
SB_BATCH = 4
LOG2E = math.log2(math.e)


def _sb_scores(qs, k_ref, ks, z_ref):
    for i, q in enumerate(qs):
        kb = k_ref[i // 2, pl.ds(ks, SB_T), :]
        z_ref[i] = lax.dot_general(q, kb, (((1,), (1,)), ((), ())), preferred_element_type=F32)


def _sb_values(v_ref, ks, a_ref, p_ref, acc_ref):
    for i in range(a_ref.shape[0]):
        vb = v_ref[i // 2, pl.ds(ks, SB_T), :]
        acc_ref[i] += p_ref[i] * jnp.dot(a_ref[i], vb, preferred_element_type=F32)


def _sb_stage1(z2, upper, causal):
    hi = jnp.maximum(z2, 0.0)
    lo = z2 - hi
    log1p = jnp.log2(1.0 + jnp.exp2(lo - hi))
    sp = hi + log1p
    if causal is not None:
        sp = jnp.where(causal, sp, 0.0)
    suffix = jnp.dot(sp.astype(BF16), upper, preferred_element_type=F32)
    return lo - log1p, suffix, suffix[:, 0:1] + sp[:, 0:1]


def _sb_step(qs, k_ref, v_ref, ks, upper, z_ref, a_ref, p_ref, c_ref, acc_ref, causal):
    t = SB_T
    ks_next = pl.multiple_of(jnp.maximum(ks - t, 0), t)
    stage1 = []
    for i, q in enumerate(qs):
        z2 = z_ref[i]
        _sb_scores(qs[i:i + 1], k_ref.at[i // 2:i // 2 + 1], ks_next, z_ref.at[i:i + 1])
        if causal is None:
            _sb_values(v_ref.at[i // 2:i // 2 + 1], pl.multiple_of(ks + t, t), a_ref.at[i:i + 1],
                       p_ref.at[i:i + 1], acc_ref.at[i:i + 1])
        stage1.append(_sb_stage1(z2, upper, causal))
    for i, (log_beta, suffix, total) in enumerate(stage1):
        neg_carry = c_ref[i]
        a = jnp.exp2(log_beta - suffix)
        if causal is not None:
            a = jnp.where(causal, a, 0.0)
        a_ref[i] = a.astype(BF16)
        p_ref[i] = jnp.exp2(neg_carry)
        c_ref[i] = neg_carry - total


def _sb_kernel(q_ref, k_ref, v_ref, o_ref, z_ref, a_ref, p_ref, c_ref, acc_ref):
    t = SB_T
    nb = q_ref.shape[0]
    qi = pl.program_id(1)
    lane = lax.broadcasted_iota(jnp.int32, (t, LANES), 1)
    first = lane < SB_HEAD_DIM
    qs = []
    for bi in range(nb):
        q2 = q_ref[bi]
        zero = jnp.zeros_like(q2)
        qs += [jnp.where(first, q2, zero), jnp.where(first, zero, q2)]
    row = lax.broadcasted_iota(jnp.int32, (t, t), 0)
    col = lax.broadcasted_iota(jnp.int32, (t, t), 1)
    upper = jnp.where(row > col, 1.0, 0.0).astype(BF16)
    causal = col < row

    c_ref[...] = jnp.zeros_like(c_ref)
    acc_ref[...] = jnp.zeros_like(acc_ref)
    diag = pl.multiple_of(qi * t, t)
    _sb_scores(qs, k_ref, diag, z_ref)
    _sb_step(qs, k_ref, v_ref, diag, upper, z_ref, a_ref, p_ref, c_ref, acc_ref, causal)

    @pl.loop(0, qi)
    def _(j):
        ks = pl.multiple_of((qi - 1 - j) * t, t)
        _sb_step(qs, k_ref, v_ref, ks, upper, z_ref, a_ref, p_ref, c_ref, acc_ref, None)

    _sb_values(v_ref, 0, a_ref, p_ref, acc_ref)
    for bi in range(nb):
        o_ref[bi] = jnp.where(first, acc_ref[2 * bi], acc_ref[2 * bi + 1]).astype(o_ref.dtype)


def _sb_attention(proj, b, s):
    t, nb = SB_T, SB_BATCH
    ns = 2 * nb
    qc, kc, vc = (3 * MIX_W) // LANES, (4 * MIX_W) // LANES, (5 * MIX_W) // LANES
    pairs = MIX_W // LANES
    return pl.pallas_call(
        _sb_kernel,
        grid=(b // nb * pairs, s // t),
        in_specs=[
            pl.BlockSpec((nb, t, LANES), lambda g, i: (g // pairs, i, qc + g % pairs)),
            pl.BlockSpec((nb, s, LANES), lambda g, i: (g // pairs, 0, kc + g % pairs)),
            pl.BlockSpec((nb, s, LANES), lambda g, i: (g // pairs, 0, vc + g % pairs)),
        ],
        out_specs=pl.BlockSpec((nb, t, LANES), lambda g, i: (g // pairs, i, g % pairs)),
        out_shape=jax.ShapeDtypeStruct((b, s, MIX_W), BF16),
        scratch_shapes=[pltpu.VMEM((ns, t, t), F32), pltpu.VMEM((ns, t, t), BF16),
                        pltpu.VMEM((ns, t, 1), F32), pltpu.VMEM((ns, t, 1), F32),
                        pltpu.VMEM((ns, t, LANES), F32)],
        compiler_params=_params(("parallel", "arbitrary")),
        name="sb_attention",
    )(proj, proj, proj)


RET_STEP_CHUNKS = 4


def _rotary(tf, cos2, sin2):
    return tf * cos2 + pltpu.roll(tf, RET_HEAD_DIM // 2, 1) * sin2


def _ret_kernel(q_ref, k_ref, v_ref, g_ref, cos_ref, sin_ref, gain_ref, dintra_ref, kdec_ref,
                qdec_ref, cdec_ref, o_ref, state_ref):
    c, d = RET_CHUNK, RET_HEAD_DIM

    @pl.when(pl.program_id(1) == 0)
    def _():
        state_ref[...] = jnp.zeros_like(state_ref)

    units = [(h, n) for h in range(RET_HEADS) for n in range(RET_STEP_CHUNKS)]
    blk = lambda ref, h, n: ref[0, pl.ds(n * c, c), pl.ds(h * d, d)]
    qb, scores, kv = {}, {}, {}
    for h, n in units:
        cos2 = cos_ref[pl.ds(n * c, c), :]
        sin2 = sin_ref[pl.ds(n * c, c), :]
        q = _rotary(blk(q_ref, h, n).astype(F32), cos2, sin2)
        k = _rotary(blk(k_ref, h, n).astype(F32), cos2, sin2) * (d ** -0.5)
        qb[h, n] = q.astype(BF16)
        scores[h, n] = lax.dot_general(qb[h, n], k.astype(BF16), (((1,), (1,)), ((), ())),
                                       preferred_element_type=F32)
        kv[h, n] = lax.dot_general((k * kdec_ref[h]).astype(BF16), blk(v_ref, h, n),
                                   (((0,), (0,)), ((), ())), preferred_element_type=F32)
    inner = {}
    for h, n in units:
        p = (scores[h, n] * dintra_ref[h]).astype(BF16)
        inner[h, n] = jnp.dot(p, blk(v_ref, h, n), preferred_element_type=F32)
    cross = {}
    for h in range(RET_HEADS):
        state = state_ref[h]
        for n in range(RET_STEP_CHUNKS):
            cross[h, n] = jnp.dot(qb[h, n], state.astype(BF16), preferred_element_type=F32)
            state = cdec_ref[h] * state + kv[h, n]
        state_ref[h] = state
    for h, n in units:
        o = inner[h, n] + cross[h, n] * qdec_ref[h]
        mu = jnp.mean(o, axis=-1, keepdims=True)
        oc = o - mu
        var = jnp.mean(oc * oc, axis=-1, keepdims=True)
        on = oc * lax.rsqrt(var + EPS) * gain_ref[:, pl.ds(h * d, d)]
        gf = blk(g_ref, h, n).astype(F32)
        o_ref[0, pl.ds(n * c, c), pl.ds(h * d, d)] = (gf * jax.nn.sigmoid(gf) * on).astype(o_ref.dtype)


def _retention_tables(s):
    pos = jnp.arange(s, dtype=F32)
    inv_freq = ROPE_BASE ** (-jnp.arange(0, RET_HEAD_DIM, 2, dtype=F32) / RET_HEAD_DIM)
    ang = pos[:, None] * inv_freq[None, :]
    cos, sin = jnp.cos(ang), jnp.sin(ang)
    cos2 = jnp.concatenate([cos, cos], axis=-1)
    sin2 = jnp.concatenate([-sin, sin], axis=-1)
    log_gamma = jnp.log1p(-jnp.exp2(-5.0 - jnp.arange(RET_HEADS, dtype=F32)))
    idx = jnp.arange(RET_CHUNK, dtype=F32)
    rel = idx[:, None] - idx[None, :]
    dintra = jnp.where(rel >= 0, jnp.exp(jnp.maximum(rel, 0.0)[None] * log_gamma[:, None, None]), 0.0)
    kdec = jnp.exp((RET_CHUNK - 1 - idx)[None] * log_gamma[:, None])
    qdec = jnp.exp((idx + 1)[None] * log_gamma[:, None])
    cdec = jnp.exp(RET_CHUNK * log_gamma)
    bc = lambda a: jnp.broadcast_to(a[:, :, None], (RET_HEADS, RET_CHUNK, RET_HEAD_DIM))
    cdec = jnp.broadcast_to(cdec[:, None, None], (RET_HEADS, 1, RET_HEAD_DIM))
    return cos2, sin2, dintra, bc(kdec), bc(qdec), cdec


def _retention(proj, gain, b, s):
    ts = RET_STEP_CHUNKS * RET_CHUNK
    c, d = RET_CHUNK, RET_HEAD_DIM
    base = (6 * MIX_W) // MIX_W
    cos2, sin2, dintra, kdec, qdec, cdec = _retention_tables(s)
    col = lambda off: pl.BlockSpec((1, ts, MIX_W), lambda bi, i: (bi, i, base + off))
    tab = pl.BlockSpec((ts, d), lambda bi, i: (i, 0))
    full = lambda shape: pl.BlockSpec(shape, lambda bi, i: (0,) * len(shape))
    return pl.pallas_call(
        _ret_kernel,
        grid=(b, s // ts),
        in_specs=[col(0), col(1), col(2), col(3), tab, tab, full((1, MIX_W)),
                  full((RET_HEADS, c, c)), full((RET_HEADS, c, d)), full((RET_HEADS, c, d)),
                  full((RET_HEADS, 1, d))],
        out_specs=pl.BlockSpec((1, ts, MIX_W), lambda bi, i: (bi, i, 0)),
        out_shape=jax.ShapeDtypeStruct((b, s, MIX_W), BF16),
        scratch_shapes=[pltpu.VMEM((RET_HEADS, d, d), F32)],
        compiler_params=_params(("parallel", "arbitrary")),
        name="retention",
    )(proj, proj, proj, proj, cos2, sin2, gain.reshape(1, MIX_W), dintra, kdec, qdec, cdec)


MERGE_TM = 512
HALO = 8


def _merge_kernel(x_ref, cb_ref, cc_ref, ch_ref, ccp_ref, chp_ref, gl0_ref, gl1_ref, gl2_ref,
                  sb_ref, ret_ref, convw_ref, convb_ref, wbr_ref, bg_ref, wo_ref, o_ref):
    tm = MERGE_TM
    gl_refs = (gl0_ref, gl1_ref, gl2_ref)
    u = cc_ref[0].astype(F32) * ch_ref[0].astype(F32)
    prev = ccp_ref[0].astype(F32) * chp_ref[0].astype(F32)
    prev = jnp.where(pl.program_id(1) == 0, 0.0, prev)
    row = lax.broadcasted_iota(jnp.int32, (tm, MIX_W), 0)
    u1 = jnp.where(row == 0, prev[HALO - 1:HALO, :], pltpu.roll(u, 1, 0))
    u2 = jnp.where(row == 0, prev[HALO - 2:HALO - 1, :],
                   jnp.where(row == 1, prev[HALO - 1:HALO, :], pltpu.roll(u, 2, 0)))
    conv = convb_ref[...] + u2 * convw_ref[0:1, :] + u1 * convw_ref[1:2, :] + u * convw_ref[2:3, :]
    conv_out = cb_ref[0].astype(F32) * conv
    branches = (conv_out.astype(BF16), sb_ref[0], ret_ref[0])
    merged = jnp.zeros((tm, D_MODEL), F32)
    for n in range(N_BRANCH):
        up = jnp.dot(branches[n], wbr_ref[n], preferred_element_type=F32)
        logits = gl_refs[n][0].astype(F32) + bg_ref[n:n + 1, :]
        merged = merged + jax.nn.sigmoid(logits) * up
    o_ref[0] = x_ref[0] + jnp.dot(merged.astype(BF16), wo_ref[...], preferred_element_type=F32)


def _merge(x, proj, sb_out, ret_out, conv_w, conv_b, w_branch, b_gate, w_o):
    b, s, d = x.shape
    tm = MERGE_TM
    w = MIX_W
    gate_blk = (10 * MIX_W) // d
    tok = lambda width, cblk: pl.BlockSpec((1, tm, width), lambda bi, i: (bi, i, cblk))
    halo = lambda cblk: pl.BlockSpec(
        (1, HALO, w), lambda bi, i: (bi, jnp.maximum(i * (tm // HALO) - 1, 0), cblk))
    full = lambda shape: pl.BlockSpec(shape, lambda bi, i: (0,) * len(shape))
    return pl.pallas_call(
        _merge_kernel,
        grid=(b, s // tm),
        in_specs=[tok(d, 0), tok(w, 0), tok(w, 1), tok(w, 2), halo(1), halo(2),
                  tok(d, gate_blk), tok(d, gate_blk + 1), tok(d, gate_blk + 2), tok(w, 0), tok(w, 0),
                  full((CONV_K, w)), full((1, w)), full((N_BRANCH, w, d)), full((N_BRANCH, d)),
                  full((d, d))],
        out_specs=tok(d, 0),
        out_shape=jax.ShapeDtypeStruct((b, s, d), F32),
        compiler_params=_params(("parallel", "parallel")),
        name="merge",
    )(x, proj, proj, proj, proj, proj, proj, proj, proj, sb_out, ret_out, conv_w,
      conv_b.reshape(1, w),
      w_branch, b_gate, w_o)


XA_TM = 512


def _xattn_kernel(x_ref, g_ref, wq_ref, kv_ref, wo_ref, o_ref):
    x = x_ref[0]
    h = _rmsnorm_f32(x, g_ref[...]).astype(BF16)
    q = jnp.dot(h, wq_ref[...], preferred_element_type=F32)
    q = (q * (XA_HEAD_DIM ** -0.5)).astype(BF16)
    outs = []
    for hd in range(XA_HEADS):
        lo = hd * XA_HEAD_DIM
        qh = q[:, lo:lo + XA_HEAD_DIM]
        kh = kv_ref[0, :, lo:lo + XA_HEAD_DIM]
        vh = kv_ref[0, :, D_MODEL + lo:D_MODEL + lo + XA_HEAD_DIM]
        sc = lax.dot_general(qh, kh, (((1,), (1,)), ((), ())), preferred_element_type=F32)
        e = jnp.exp(sc - jnp.max(sc, axis=-1, keepdims=True))
        p = e / jnp.sum(e, axis=-1, keepdims=True)
        outs.append(jnp.dot(p.astype(BF16), vh, preferred_element_type=F32).astype(BF16))
    o = jnp.concatenate(outs, axis=-1)
    o_ref[0] = x + jnp.dot(o, wo_ref[...], preferred_element_type=F32)


def _xattn(x, g, w_q, kv, w_o):
    b, s, d = x.shape
    tm = XA_TM
    full = lambda shape: pl.BlockSpec(shape, lambda bi, i: (0,) * len(shape))
    return pl.pallas_call(
        _xattn_kernel,
        grid=(b, s // tm),
        in_specs=[pl.BlockSpec((1, tm, d), lambda bi, i: (bi, i, 0)), full((1, d)), full((d, d)),
                  pl.BlockSpec((1, MEM_LEN, 2 * d), lambda bi, i: (bi, 0, 0)), full((d, d))],
        out_specs=pl.BlockSpec((1, tm, d), lambda bi, i: (bi, i, 0)),
        out_shape=jax.ShapeDtypeStruct((b, s, d), F32),
        compiler_params=_params(("parallel", "parallel")),
        name="xattn",
    )(x, g.reshape(1, d), w_q, kv, w_o)


MLP_TM = 1024
MLP_TF = 1024


def _mlp_kernel(x_ref, g_ref, wu_ref, wd_ref, fg_ref, o_ref, h_ref, acc_ref, *, final_norm):
    j = pl.program_id(1)

    @pl.when(j == 0)
    def _():
        h_ref[...] = _rmsnorm_f32(x_ref[...], g_ref[...]).astype(BF16)
        acc_ref[...] = x_ref[...]

    rows = [pl.ds(r * (MLP_TM // ROW_SPLIT), MLP_TM // ROW_SPLIT) for r in range(ROW_SPLIT)]
    ups = [jnp.dot(h_ref[r, :], wu_ref[...], preferred_element_type=F32) for r in rows]
    acts = [jnp.square(jnp.maximum(up, 0.0)).astype(BF16) for up in ups]
    for r, act in zip(rows, acts):
        acc_ref[r, :] += jnp.dot(act, wd_ref[...], preferred_element_type=F32)

    @pl.when(j == pl.num_programs(1) - 1)
    def _():
        y = acc_ref[...]
        o_ref[...] = _rmsnorm_f32(y, fg_ref[...]) if final_norm else y


def _mlp(x2d, g, w_up, w_down, final_g, final_norm):
    m, d = x2d.shape
    f = w_up.shape[1]
    tm, tf = MLP_TM, MLP_TF
    return pl.pallas_call(
        functools.partial(_mlp_kernel, final_norm=final_norm),
        grid=(m // tm, f // tf),
        in_specs=[pl.BlockSpec((tm, d), lambda i, j: (i, 0)),
                  pl.BlockSpec((1, d), lambda i, j: (0, 0)),
                  pl.BlockSpec((d, tf), lambda i, j: (0, j)),
                  pl.BlockSpec((tf, d), lambda i, j: (j, 0)),
                  pl.BlockSpec((1, d), lambda i, j: (0, 0))],
        out_specs=pl.BlockSpec((tm, d), lambda i, j: (i, 0)),
        out_shape=jax.ShapeDtypeStruct((m, d), F32),
        scratch_shapes=[pltpu.VMEM((tm, d), BF16), pltpu.VMEM((tm, d), F32)],
        compiler_params=_params(("parallel", "arbitrary")),
        name="mlp",
    )(x2d, g.reshape(1, d), w_up, w_down, final_g.reshape(1, d))


def kernel(x, mem, norm_mix_g, w_in, b_gate, conv_w, conv_b, ret_norm_g, w_branch, w_o,
           norm_xa_g, norm_mem_g, w_xq, w_xkv, w_xo, norm_mlp_g, w_up, w_down, final_g):
    b, s, d = x.shape
    depth = w_in.shape[0]
    assert d == D_MODEL and s % SB_T == 0 and s % MERGE_TM == 0 and (b * s) % MLP_TM == 0
    assert b % SB_BATCH == 0
    mem2d = mem.reshape(b * MEM_LEN, d)
    col_scale = jnp.ones((IN_COLS,), F32).at[3 * MIX_W:4 * MIX_W].set(SB_HEAD_DIM ** -0.5 * LOG2E)
    for l in range(depth):
        w_in_l = (w_in[l] * col_scale).astype(BF16)
        proj = _norm_matmul(x.reshape(b * s, d), norm_mix_g[l], w_in_l, 1024, 2048)
        proj = proj.reshape(b, s, IN_COLS)
        sb_out = _sb_attention(proj, b, s)
        ret_out = _retention(proj, ret_norm_g[l], b, s)
        x = _merge(x, proj, sb_out, ret_out, conv_w[l], conv_b[l], w_branch[l].astype(BF16),
                   b_gate[l], w_o[l].astype(BF16))
        kv = _norm_matmul(mem2d, norm_mem_g[l], w_xkv[l].astype(BF16), b * MEM_LEN, 1024)
        x = _xattn(x, norm_xa_g[l], w_xq[l].astype(BF16), kv.reshape(b, MEM_LEN, 2 * d),
                   w_xo[l].astype(BF16))
        x = _mlp(x.reshape(b * s, d), norm_mlp_g[l], w_up[l].astype(BF16), w_down[l].astype(BF16),
                 final_g, l == depth - 1).reshape(b, s, d)
    return x
```

```python
import functools
import math

import jax
import jax.numpy as jnp
from jax import lax
from jax.experimental import pallas as pl
from jax.experimental.pallas import tpu as pltpu

D_MODEL = 1024
MEM_LEN = 256
N_BRANCH = 3
MIX_W = D_MODEL // 2
CONV_K = 3
SB_HEADS = 8
SB_HEAD_DIM = MIX_W // SB_HEADS
RET_HEADS = 4
RET_HEAD_DIM = MIX_W // RET_HEADS
RET_CHUNK = 128
XA_HEADS = 4
XA_HEAD_DIM = D_MODEL // XA_HEADS
D_FF = 4 * D_MODEL
ROPE_BASE = 10000.0
EPS = 1e-6
IN_COLS = 10 * MIX_W + N_BRANCH * D_MODEL

LANES = 128
ROW_SPLIT = 2
VMEM_LIMIT = 48 * 1024 * 1024

F32 = jnp.float32
BF16 = jnp.bfloat16


def _params(semantics):
    return pltpu.CompilerParams(dimension_semantics=semantics, vmem_limit_bytes=VMEM_LIMIT)


def _rmsnorm_f32(xf, g):
    return xf * lax.rsqrt(jnp.mean(xf * xf, axis=-1, keepdims=True) + EPS) * g


def _norm_matmul_kernel(x_ref, g_ref, w_ref, o_ref, h_ref):
    @pl.when(pl.program_id(1) == 0)
    def _():
        h_ref[...] = _rmsnorm_f32(x_ref[...], g_ref[...]).astype(BF16)

    o_ref[...] = jnp.dot(h_ref[...], w_ref[...], preferred_element_type=F32).astype(o_ref.dtype)


def _norm_matmul(x2d, g, w, tm, tn):
    m, d = x2d.shape
    n = w.shape[1]
    return pl.pallas_call(
        _norm_matmul_kernel,
        grid=(m // tm, n // tn),
        in_specs=[
            pl.BlockSpec((tm, d), lambda i, j: (i, 0)),
            pl.BlockSpec((1, d), lambda i, j: (0, 0)),
            pl.BlockSpec((d, tn), lambda i, j: (0, j)),
        ],
        out_specs=pl.BlockSpec((tm, tn), lambda i, j: (i, j)),
        out_shape=jax.ShapeDtypeStruct((m, n), BF16),
        scratch_shapes=[pltpu.VMEM((tm, d), BF16)],
        compiler_params=_params(("parallel", "arbitrary")),
        name="norm_matmul",
    )(x2d, g.reshape(1, d), w)


IN_TM = 1024
IN_TN = 4 * MIX_W
assert IN_COLS == 4 * IN_TN


def _rotary(tf, cos2, sin2):
    return tf * cos2 + pltpu.roll(tf, RET_HEAD_DIM // 2, 1) * sin2


def _in_proj_kernel(x_ref, g_ref, w_ref, bias_ref, cos_ref, sin_ref, o_ref, h_ref):
    j = pl.program_id(1)
    w = MIX_W

    def project():
        return jnp.dot(h_ref[...], w_ref[...], preferred_element_type=F32)

    @pl.when(j == 0)
    def _():
        h_ref[...] = _rmsnorm_f32(x_ref[...], g_ref[...]).astype(BF16)
        o_ref[...] = project().astype(BF16)

    @pl.when(j == 1)
    def _():
        acc = project()
        o_ref[:, :2 * w] = acc[:, :2 * w].astype(BF16)
        cos2, sin2 = cos_ref[...], sin_ref[...]
        for start, scale in ((2 * w, None), (3 * w, RET_HEAD_DIM ** -0.5)):
            for h in range(RET_HEADS):
                cols = slice(start + h * RET_HEAD_DIM, start + (h + 1) * RET_HEAD_DIM)
                r = _rotary(acc[:, cols], cos2, sin2)
                o_ref[:, cols] = (r if scale is None else r * scale).astype(BF16)

    @pl.when(j == 2)
    def _():
        acc = project()
        o_ref[:, :w] = acc[:, :w].astype(BF16)
        gate = acc[:, w:2 * w]
        o_ref[:, w:2 * w] = (gate * jax.nn.sigmoid(gate)).astype(BF16)
        o_ref[:, 2 * w:] = jax.nn.sigmoid(acc[:, 2 * w:] + bias_ref[:, 2 * w:]).astype(BF16)

    @pl.when(j == 3)
    def _():
        o_ref[...] = jax.nn.sigmoid(project() + bias_ref[...]).astype(BF16)


def _in_proj(x2d, g, w, b_gate, cos2, sin2, s):
    m, d = x2d.shape
    tm, tn = IN_TM, IN_TN
    bias = jnp.concatenate([jnp.zeros((IN_COLS - N_BRANCH * D_MODEL,), F32), b_gate.reshape(-1)])
    seq_blocks = s // tm
    return pl.pallas_call(
        _in_proj_kernel,
        grid=(m // tm, IN_COLS // tn),
        in_specs=[
            pl.BlockSpec((tm, d), lambda i, j: (i, 0)),
            pl.BlockSpec((1, d), lambda i, j: (0, 0)),
            pl.BlockSpec((d, tn), lambda i, j: (0, j)),
            pl.BlockSpec((1, tn), lambda i, j: (0, j)),
            pl.BlockSpec((tm, RET_HEAD_DIM), lambda i, j: (i % seq_blocks, 0)),
            pl.BlockSpec((tm, RET_HEAD_DIM), lambda i, j: (i % seq_blocks, 0)),
        ],
        out_specs=pl.BlockSpec((tm, tn), lambda i, j: (i, j)),
        out_shape=jax.ShapeDtypeStruct((m, IN_COLS), BF16),
        scratch_shapes=[pltpu.VMEM((tm, d), BF16)],
        compiler_params=_params(("parallel", "arbitrary")),
        name="in_proj",
    )(x2d, g.reshape(1, d), w, bias.reshape(1, IN_COLS), cos2, sin2)


SB_T = 256
SB_BATCH = 4
LOG2E = math.log2(math.e)


def _sb_scores(qs, k_ref, ks, z_ref):
    for i, q in enumerate(qs):
        kb = k_ref[i // 2, pl.ds(ks, SB_T), :]
        z_ref[i] = lax.dot_general(q, kb, (((1,), (1,)), ((), ())), preferred_element_type=F32)


def _sb_values(v_ref, ks, a_ref, p_ref, acc_ref):
    for i in range(a_ref.shape[0]):
        vb = v_ref[i // 2, pl.ds(ks, SB_T), :]
        acc_ref[i] += p_ref[i] * jnp.dot(a_ref[i], vb, preferred_element_type=F32)


def _sb_stage1(z2, upper):
    zb = z2.astype(BF16)
    hi = jnp.maximum(zb, 0.0)
    lo = zb - hi
    log1p = jnp.log(1.0 + jnp.exp2(lo - hi)) * jnp.asarray(LOG2E, BF16)
    sp = hi + log1p
    suffix = jnp.dot(sp, upper, preferred_element_type=F32)
    return (lo - log1p).astype(F32), suffix, suffix[:, 0:1] + sp[:, 0:1].astype(F32)


def _sb_step(qs, k_ref, v_ref, ks, upper, z_ref, a_ref, p_ref, c_ref, acc_ref):
    t = SB_T
    last = v_ref.shape[1] - t
    ks_next = pl.multiple_of(jnp.maximum(ks - t, 0), t)
    ks_prev = pl.multiple_of(jnp.minimum(ks + t, last), t)
    stage1 = []
    for i, q in enumerate(qs):
        z2 = z_ref[i]
        _sb_scores(qs[i:i + 1], k_ref.at[i // 2:i // 2 + 1], ks_next, z_ref.at[i:i + 1])
        _sb_values(v_ref.at[i // 2:i // 2 + 1], ks_prev, a_ref.at[i:i + 1], p_ref.at[i:i + 1],
                   acc_ref.at[i:i + 1])
        stage1.append(_sb_stage1(z2, upper))
    for i, (log_beta, suffix, total) in enumerate(stage1):
        neg_carry = c_ref[i]
        a_ref[i] = jnp.exp2(log_beta - suffix).astype(BF16)
        p_ref[i] = jnp.exp2(neg_carry)
        c_ref[i] = neg_carry - total


def _sb_kernel(q_ref, k_ref, v_ref, o_ref, z_ref, a_ref, p_ref, c_ref, acc_ref):
    t = SB_T
    nb = q_ref.shape[0]
    qi = pl.program_id(1)
    lane = lax.broadcasted_iota(jnp.int32, (t, LANES), 1)
    first = lane < SB_HEAD_DIM
    qs = []
    for bi in range(nb):
        q2 = q_ref[bi]
        zero = jnp.zeros_like(q2)
        qs += [jnp.where(first, q2, zero), jnp.where(first, zero, q2)]
    row = lax.broadcasted_iota(jnp.int32, (t, t), 0)
    col = lax.broadcasted_iota(jnp.int32, (t, t), 1)
    upper = jnp.where(row > col, 1.0, 0.0).astype(BF16)
    causal = col < row

    for ref in (a_ref, p_ref, c_ref, acc_ref):
        ref[...] = jnp.zeros_like(ref)
    diag = pl.multiple_of(qi * t, t)
    for i, q in enumerate(qs):
        z = lax.dot_general(q, k_ref[i // 2, pl.ds(diag, t), :], (((1,), (1,)), ((), ())),
                            preferred_element_type=F32)
        z_ref[i] = jnp.where(causal, z, -jnp.inf)

    @pl.loop(0, qi + 1)
    def _(j):
        ks = pl.multiple_of((qi - j) * t, t)
        _sb_step(qs, k_ref, v_ref, ks, upper, z_ref, a_ref, p_ref, c_ref, acc_ref)

    _sb_values(v_ref, 0, a_ref, p_ref, acc_ref)
    for bi in range(nb):
        o_ref[bi] = jnp.where(first, acc_ref[2 * bi], acc_ref[2 * bi + 1]).astype(o_ref.dtype)


def _sb_attention(proj, b, s):
    t, nb = SB_T, SB_BATCH
    ns = 2 * nb
    qc, kc, vc = (3 * MIX_W) // LANES, (4 * MIX_W) // LANES, (5 * MIX_W) // LANES
    pairs = MIX_W // LANES
    return pl.pallas_call(
        _sb_kernel,
        grid=(b // nb * pairs, s // t),
        in_specs=[
            pl.BlockSpec((nb, t, LANES), lambda g, i: (g // pairs, i, qc + g % pairs)),
            pl.BlockSpec((nb, s, LANES), lambda g, i: (g // pairs, 0, kc + g % pairs)),
            pl.BlockSpec((nb, s, LANES), lambda g, i: (g // pairs, 0, vc + g % pairs)),
        ],
        out_specs=pl.BlockSpec((nb, t, LANES), lambda g, i: (g // pairs, i, g % pairs)),
        out_shape=jax.ShapeDtypeStruct((b, s, MIX_W), BF16),
        scratch_shapes=[pltpu.VMEM((ns, t, t), F32), pltpu.VMEM((ns, t, t), BF16),
                        pltpu.VMEM((ns, t, 1), F32), pltpu.VMEM((ns, t, 1), F32),
                        pltpu.VMEM((ns, t, LANES), F32)],
        compiler_params=_params(("parallel", "arbitrary")),
        name="sb_attention",
    )(proj, proj, proj)


RET_STEP_CHUNKS = 4


def _ret_kernel(q_ref, k_ref, v_ref, g_ref, gain_ref, dintra_ref, kdec_ref, qdec_ref, cdec_ref,
                o_ref, state_ref):
    c, d = RET_CHUNK, RET_HEAD_DIM

    @pl.when(pl.program_id(1) == 0)
    def _():
        state_ref[...] = jnp.zeros_like(state_ref)

    units = [(h, n) for h in range(RET_HEADS) for n in range(RET_STEP_CHUNKS)]
    blk = lambda ref, h, n: ref[0, pl.ds(n * c, c), pl.ds(h * d, d)]
    qb, scores, kv = {}, {}, {}
    for h, n in units:
        qb[h, n] = blk(q_ref, h, n)
        k = blk(k_ref, h, n)
        scores[h, n] = lax.dot_general(qb[h, n], k, (((1,), (1,)), ((), ())),
                                       preferred_element_type=F32)
        kv[h, n] = lax.dot_general((k.astype(F32) * kdec_ref[h]).astype(BF16), blk(v_ref, h, n),
                                   (((0,), (0,)), ((), ())), preferred_element_type=F32)
    inner = {}
    for h, n in units:
        p = (scores[h, n] * dintra_ref[h]).astype(BF16)
        inner[h, n] = jnp.dot(p, blk(v_ref, h, n), preferred_element_type=F32)
    cross = {}
    for h in range(RET_HEADS):
        state = state_ref[h]
        for n in range(RET_STEP_CHUNKS):
            cross[h, n] = jnp.dot(qb[h, n], state.astype(BF16), preferred_element_type=F32)
            state = cdec_ref[h] * state + kv[h, n]
        state_ref[h] = state
    for h, n in units:
        o = inner[h, n] + cross[h, n] * qdec_ref[h]
        mu = jnp.mean(o, axis=-1, keepdims=True)
        oc = o - mu
        var = jnp.mean(oc * oc, axis=-1, keepdims=True)
        on = oc * lax.rsqrt(var + EPS) * gain_ref[:, pl.ds(h * d, d)]
        o_ref[0, pl.ds(n * c, c), pl.ds(h * d, d)] = (
            blk(g_ref, h, n).astype(F32) * on).astype(o_ref.dtype)


def _rotary_tables(s):
    pos = jnp.arange(s, dtype=F32)
    inv_freq = ROPE_BASE ** (-jnp.arange(0, RET_HEAD_DIM, 2, dtype=F32) / RET_HEAD_DIM)
    ang = pos[:, None] * inv_freq[None, :]
    cos, sin = jnp.cos(ang), jnp.sin(ang)
    return jnp.concatenate([cos, cos], axis=-1), jnp.concatenate([-sin, sin], axis=-1)


def _retention_tables():
    log_gamma = jnp.log1p(-jnp.exp2(-5.0 - jnp.arange(RET_HEADS, dtype=F32)))
    idx = jnp.arange(RET_CHUNK, dtype=F32)
    rel = idx[:, None] - idx[None, :]
    dintra = jnp.where(rel >= 0, jnp.exp(jnp.maximum(rel, 0.0)[None] * log_gamma[:, None, None]), 0.0)
    kdec = jnp.exp((RET_CHUNK - 1 - idx)[None] * log_gamma[:, None])
    qdec = jnp.exp((idx + 1)[None] * log_gamma[:, None])
    cdec = jnp.exp(RET_CHUNK * log_gamma)
    bc = lambda a: jnp.broadcast_to(a[:, :, None], (RET_HEADS, RET_CHUNK, RET_HEAD_DIM))
    cdec = jnp.broadcast_to(cdec[:, None, None], (RET_HEADS, 1, RET_HEAD_DIM))
    return dintra, bc(kdec), bc(qdec), cdec


def _retention(proj, gain, b, s):
    ts = RET_STEP_CHUNKS * RET_CHUNK
    c, d = RET_CHUNK, RET_HEAD_DIM
    base = (6 * MIX_W) // MIX_W
    dintra, kdec, qdec, cdec = _retention_tables()
    col = lambda off: pl.BlockSpec((1, ts, MIX_W), lambda bi, i: (bi, i, base + off))
    full = lambda shape: pl.BlockSpec(shape, lambda bi, i: (0,) * len(shape))
    return pl.pallas_call(
        _ret_kernel,
        grid=(b, s // ts),
        in_specs=[col(0), col(1), col(2), col(3), full((1, MIX_W)),
                  full((RET_HEADS, c, c)), full((RET_HEADS, c, d)), full((RET_HEADS, c, d)),
                  full((RET_HEADS, 1, d))],
        out_specs=pl.BlockSpec((1, ts, MIX_W), lambda bi, i: (bi, i, 0)),
        out_shape=jax.ShapeDtypeStruct((b, s, MIX_W), BF16),
        scratch_shapes=[pltpu.VMEM((RET_HEADS, d, d), F32)],
        compiler_params=_params(("parallel", "arbitrary")),
        name="retention",
    )(proj, proj, proj, proj, gain.reshape(1, MIX_W), dintra, kdec, qdec, cdec)


MERGE_TM = 512
HALO = 8


def _merge_kernel(x_ref, cb_ref, cc_ref, ch_ref, ccp_ref, chp_ref, gate0_ref, gate1_ref,
                  gate2_ref, sb_ref, ret_ref, convw_ref, convb_ref, wbr_ref, wo_ref, o_ref):
    tm = MERGE_TM
    gate_refs = (gate0_ref, gate1_ref, gate2_ref)
    u = cc_ref[0].astype(F32) * ch_ref[0].astype(F32)
    prev = ccp_ref[0].astype(F32) * chp_ref[0].astype(F32)
    prev = jnp.where(pl.program_id(1) == 0, 0.0, prev)
    row = lax.broadcasted_iota(jnp.int32, (tm, MIX_W), 0)
    u1 = jnp.where(row == 0, prev[HALO - 1:HALO, :], pltpu.roll(u, 1, 0))
    u2 = jnp.where(row == 0, prev[HALO - 2:HALO - 1, :],
                   jnp.where(row == 1, prev[HALO - 1:HALO, :], pltpu.roll(u, 2, 0)))
    conv = convb_ref[...] + u2 * convw_ref[0:1, :] + u1 * convw_ref[1:2, :] + u * convw_ref[2:3, :]
    conv_out = cb_ref[0].astype(F32) * conv
    branches = (conv_out.astype(BF16), sb_ref[0], ret_ref[0])
    merged = jnp.zeros((tm, D_MODEL), F32)
    for n in range(N_BRANCH):
        up = jnp.dot(branches[n], wbr_ref[n], preferred_element_type=F32)
        merged = merged + gate_refs[n][0].astype(F32) * up
    o_ref[0] = x_ref[0] + jnp.dot(merged.astype(BF16), wo_ref[...], preferred_element_type=F32)


def _merge(x, proj, sb_out, ret_out, conv_w, conv_b, w_branch, w_o):
    b, s, d = x.shape
    tm = MERGE_TM
    w = MIX_W
    gate_blk = (10 * MIX_W) // d
    tok = lambda width, cblk: pl.BlockSpec((1, tm, width), lambda bi, i: (bi, i, cblk))
    halo = lambda cblk: pl.BlockSpec(
        (1, HALO, w), lambda bi, i: (bi, jnp.maximum(i * (tm // HALO) - 1, 0), cblk))
    full = lambda shape: pl.BlockSpec(shape, lambda bi, i: (0,) * len(shape))
    return pl.pallas_call(
        _merge_kernel,
        grid=(b, s // tm),
        in_specs=[tok(d, 0), tok(w, 0), tok(w, 1), tok(w, 2), halo(1), halo(2),
                  tok(d, gate_blk), tok(d, gate_blk + 1), tok(d, gate_blk + 2), tok(w, 0), tok(w, 0),
                  full((CONV_K, w)), full((1, w)), full((N_BRANCH, w, d)), full((d, d))],
        out_specs=tok(d, 0),
        out_shape=jax.ShapeDtypeStruct((b, s, d), F32),
        compiler_params=_params(("parallel", "parallel")),
        name="merge",
    )(x, proj, proj, proj, proj, proj, proj, proj, proj, sb_out, ret_out, conv_w,
      conv_b.reshape(1, w), w_branch, w_o)


XA_TM = 512


def _xattn_kernel(x_ref, g_ref, wq_ref, kv_ref, wo_ref, o_ref):
    x = x_ref[0]
    h = _rmsnorm_f32(x, g_ref[...]).astype(BF16)
    q = jnp.dot(h, wq_ref[...], preferred_element_type=F32)
    q = (q * (XA_HEAD_DIM ** -0.5)).astype(BF16)
    outs = []
    for hd in range(XA_HEADS):
        lo = hd * XA_HEAD_DIM
        qh = q[:, lo:lo + XA_HEAD_DIM]
        kh = kv_ref[0, :, lo:lo + XA_HEAD_DIM]
        vh = kv_ref[0, :, D_MODEL + lo:D_MODEL + lo + XA_HEAD_DIM]
        sc = lax.dot_general(qh, kh, (((1,), (1,)), ((), ())), preferred_element_type=F32)
        e = jnp.exp(sc - jnp.max(sc, axis=-1, keepdims=True))
        p = e / jnp.sum(e, axis=-1, keepdims=True)
        outs.append(jnp.dot(p.astype(BF16), vh, preferred_element_type=F32).astype(BF16))
    o = jnp.concatenate(outs, axis=-1)
    o_ref[0] = x + jnp.dot(o, wo_ref[...], preferred_element_type=F32)


def _xattn(x, g, w_q, kv, w_o):
    b, s, d = x.shape
    tm = XA_TM
    full = lambda shape: pl.BlockSpec(shape, lambda bi, i: (0,) * len(shape))
    return pl.pallas_call(
        _xattn_kernel,
        grid=(b, s // tm),
        in_specs=[pl.BlockSpec((1, tm, d), lambda bi, i: (bi, i, 0)), full((1, d)), full((d, d)),
                  pl.BlockSpec((1, MEM_LEN, 2 * d), lambda bi, i: (bi, 0, 0)), full((d, d))],
        out_specs=pl.BlockSpec((1, tm, d), lambda bi, i: (bi, i, 0)),
        out_shape=jax.ShapeDtypeStruct((b, s, d), F32),
        compiler_params=_params(("parallel", "parallel")),
        name="xattn",
    )(x, g.reshape(1, d), w_q, kv, w_o)


MLP_TM = 1024
MLP_TF = 1024


def _mlp_kernel(x_ref, g_ref, wu_ref, wd_ref, fg_ref, o_ref, h_ref, acc_ref, *, final_norm):
    j = pl.program_id(1)

    @pl.when(j == 0)
    def _():
        h_ref[...] = _rmsnorm_f32(x_ref[...], g_ref[...]).astype(BF16)
        acc_ref[...] = x_ref[...]

    rows = [pl.ds(r * (MLP_TM // ROW_SPLIT), MLP_TM // ROW_SPLIT) for r in range(ROW_SPLIT)]
    ups = [jnp.dot(h_ref[r, :], wu_ref[...], preferred_element_type=F32) for r in rows]
    acts = [jnp.square(jnp.maximum(up, 0.0)).astype(BF16) for up in ups]
    for r, act in zip(rows, acts):
        acc_ref[r, :] += jnp.dot(act, wd_ref[...], preferred_element_type=F32)

    @pl.when(j == pl.num_programs(1) - 1)
    def _():
        y = acc_ref[...]
        o_ref[...] = _rmsnorm_f32(y, fg_ref[...]) if final_norm else y


def _mlp(x2d, g, w_up, w_down, final_g, final_norm):
    m, d = x2d.shape
    f = w_up.shape[1]
    tm, tf = MLP_TM, MLP_TF
    return pl.pallas_call(
        functools.partial(_mlp_kernel, final_norm=final_norm),
        grid=(m // tm, f // tf),
        in_specs=[pl.BlockSpec((tm, d), lambda i, j: (i, 0)),
                  pl.BlockSpec((1, d), lambda i, j: (0, 0)),
                  pl.BlockSpec((d, tf), lambda i, j: (0, j)),
                  pl.BlockSpec((tf, d), lambda i, j: (j, 0)),
                  pl.BlockSpec((1, d), lambda i, j: (0, 0))],
        out_specs=pl.BlockSpec((tm, d), lambda i, j: (i, 0)),
        out_shape=jax.ShapeDtypeStruct((m, d), F32),
        scratch_shapes=[pltpu.VMEM((tm, d), BF16), pltpu.VMEM((tm, d), F32)],
        compiler_params=_params(("parallel", "arbitrary")),
        name="mlp",
    )(x2d, g.reshape(1, d), w_up, w_down, final_g.reshape(1, d))


def kernel(x, mem, norm_mix_g, w_in, b_gate, conv_w, conv_b, ret_norm_g, w_branch, w_o,
           norm_xa_g, norm_mem_g, w_xq, w_xkv, w_xo, norm_mlp_g, w_up, w_down, final_g):
    b, s, d = x.shape
    depth = w_in.shape[0]
    assert d == D_MODEL and s % SB_T == 0 and s % MERGE_TM == 0 and (b * s) % MLP_TM == 0
    assert b % SB_BATCH == 0
    mem2d = mem.reshape(b * MEM_LEN, d)
    col_scale = jnp.ones((IN_COLS,), F32).at[3 * MIX_W:4 * MIX_W].set(SB_HEAD_DIM ** -0.5 * LOG2E)
    cos2, sin2 = _rotary_tables(s)
    for l in range(depth):
        w_in_l = (w_in[l] * col_scale).astype(BF16)
        proj = _in_proj(x.reshape(b * s, d), norm_mix_g[l], w_in_l, b_gate[l], cos2, sin2, s)
        proj = proj.reshape(b, s, IN_COLS)
        sb_out = _sb_attention(proj, b, s)
        ret_out = _retention(proj, ret_norm_g[l], b, s)
        x = _merge(x, proj, sb_out, ret_out, conv_w[l], conv_b[l], w_branch[l].astype(BF16),
                   w_o[l].astype(BF16))
        kv = _norm_matmul(mem2d, norm_mem_g[l], w_xkv[l].astype(BF16), b * MEM_LEN, 1024)
        x = _xattn(x, norm_xa_g[l], w_xq[l].astype(BF16), kv.reshape(b, MEM_LEN, 2 * d),
                   w_xo[l].astype(BF16))
        x = _mlp(x.reshape(b * s, d), norm_mlp_g[l], w_up[l].astype(BF16), w_down[l].astype(BF16),
                 final_g, l == depth - 1).reshape(b, s, d)
    return x
```

```python
import functools
import math

import jax
import jax.numpy as jnp
from jax import lax
from jax.experimental import pallas as pl
from jax.experimental.pallas import tpu as pltpu

D_MODEL = 1024
MEM_LEN = 256
N_BRANCH = 3
MIX_W = D_MODEL // 2
CONV_K = 3
SB_HEADS = 8
SB_HEAD_DIM = MIX_W // SB_HEADS
RET_HEADS = 4
RET_HEAD_DIM = MIX_W // RET_HEADS
RET_CHUNK = 128
XA_HEADS = 4
XA_HEAD_DIM = D_MODEL // XA_HEADS
D_FF = 4 * D_MODEL
ROPE_BASE = 10000.0
EPS = 1e-6
IN_COLS = 10 * MIX_W + N_BRANCH * D_MODEL

LANES = 128
ROW_SPLIT = 2
VMEM_LIMIT = 48 * 1024 * 1024

F32 = jnp.float32
BF16 = jnp.bfloat16


def _params(semantics):
    return pltpu.CompilerParams(dimension_semantics=semantics, vmem_limit_bytes=VMEM_LIMIT)


def _rmsnorm_f32(xf, g):
    return xf * lax.rsqrt(jnp.mean(xf * xf, axis=-1, keepdims=True) + EPS) * g


def _norm_matmul_kernel(x_ref, g_ref, w_ref, o_ref, h_ref):
    @pl.when(pl.program_id(1) == 0)
    def _():
        h_ref[...] = _rmsnorm_f32(x_ref[...], g_ref[...]).astype(BF16)

    o_ref[...] = jnp.dot(h_ref[...], w_ref[...], preferred_element_type=F32).astype(o_ref.dtype)


def _norm_matmul(x2d, g, w, tm, tn):
    m, d = x2d.shape
    n = w.shape[1]
    return pl.pallas_call(
        _norm_matmul_kernel,
        grid=(m // tm, n // tn),
        in_specs=[
            pl.BlockSpec((tm, d), lambda i, j: (i, 0)),
            pl.BlockSpec((1, d), lambda i, j: (0, 0)),
            pl.BlockSpec((d, tn), lambda i, j: (0, j)),
        ],
        out_specs=pl.BlockSpec((tm, tn), lambda i, j: (i, j)),
        out_shape=jax.ShapeDtypeStruct((m, n), BF16),
        scratch_shapes=[pltpu.VMEM((tm, d), BF16)],
        compiler_params=_params(("parallel", "arbitrary")),
        name="norm_matmul",
    )(x2d, g.reshape(1, d), w)


IN_TM = 1024
IN_TN = 4 * MIX_W
assert IN_COLS == 4 * IN_TN


def _rotary(tf, cos2, sin2):
    return tf * cos2 + pltpu.roll(tf, RET_HEAD_DIM // 2, 1) * sin2


def _in_proj_kernel(x_ref, g_ref, w_ref, bias_ref, cos_ref, sin_ref, o_ref, h_ref):
    j = pl.program_id(1)
    w = MIX_W
    groups = [pl.ds(r * (IN_TM // ROW_SPLIT), IN_TM // ROW_SPLIT) for r in range(ROW_SPLIT)]

    def project():
        return [(rows, jnp.dot(h_ref[rows, :], w_ref[...], preferred_element_type=F32))
                for rows in groups]

    @pl.when(j == 0)
    def _():
        for rows in groups:
            h_ref[rows, :] = _rmsnorm_f32(x_ref[rows, :], g_ref[...]).astype(BF16)
        for rows, acc in project():
            o_ref[rows, :] = acc.astype(BF16)

    @pl.when(j == 1)
    def _():
        for rows, acc in project():
            o_ref[rows, :2 * w] = acc[:, :2 * w].astype(BF16)
            cos2, sin2 = cos_ref[rows, :], sin_ref[rows, :]
            for start, scale in ((2 * w, None), (3 * w, RET_HEAD_DIM ** -0.5)):
                for h in range(RET_HEADS):
                    cols = slice(start + h * RET_HEAD_DIM, start + (h + 1) * RET_HEAD_DIM)
                    r = _rotary(acc[:, cols], cos2, sin2)
                    o_ref[rows, cols] = (r if scale is None else r * scale).astype(BF16)

    @pl.when(j == 2)
    def _():
        for rows, acc in project():
            o_ref[rows, :w] = acc[:, :w].astype(BF16)
            gate = acc[:, w:2 * w]
            o_ref[rows, w:2 * w] = (gate * jax.nn.sigmoid(gate)).astype(BF16)
            o_ref[rows, 2 * w:] = jax.nn.sigmoid(acc[:, 2 * w:] + bias_ref[:, 2 * w:]).astype(BF16)

    @pl.when(j == 3)
    def _():
        for rows, acc in project():
            o_ref[rows, :] = jax.nn.sigmoid(acc + bias_ref[...]).astype(BF16)


def _in_proj(x2d, g, w, b_gate, cos2, sin2, s):
    m, d = x2d.shape
    tm, tn = IN_TM, IN_TN
    bias = jnp.concatenate([jnp.zeros((IN_COLS - N_BRANCH * D_MODEL,), F32), b_gate.reshape(-1)])
    seq_blocks = s // tm
    return pl.pallas_call(
        _in_proj_kernel,
        grid=(m // tm, IN_COLS // tn),
        in_specs=[
            pl.BlockSpec((tm, d), lambda i, j: (i, 0)),
            pl.BlockSpec((1, d), lambda i, j: (0, 0)),
            pl.BlockSpec((d, tn), lambda i, j: (0, j)),
            pl.BlockSpec((1, tn), lambda i, j: (0, j)),
            pl.BlockSpec((tm, RET_HEAD_DIM), lambda i, j: (i % seq_blocks, 0)),
            pl.BlockSpec((tm, RET_HEAD_DIM), lambda i, j: (i % seq_blocks, 0)),
        ],
        out_specs=pl.BlockSpec((tm, tn), lambda i, j: (i, j)),
        out_shape=jax.ShapeDtypeStruct((m, IN_COLS), BF16),
        scratch_shapes=[pltpu.VMEM((tm, d), BF16)],
        compiler_params=_params(("parallel", "arbitrary")),
        name="in_proj",
    )(x2d, g.reshape(1, d), w, bias.reshape(1, IN_COLS), cos2, sin2)


SB_T = 256
SB_BATCH = 4
LOG2E = math.log2(math.e)


def _sb_scores(qs, k_ref, ks, z_ref):
    for i, q in enumerate(qs):
        kb = k_ref[i // 2, pl.ds(ks, SB_T), :]
        z_ref[i] = lax.dot_general(q, kb, (((1,), (1,)), ((), ())), preferred_element_type=F32)


def _sb_values(v_ref, ks, a_ref, p_ref, acc_ref):
    for i in range(a_ref.shape[0]):
        vb = v_ref[i // 2, pl.ds(ks, SB_T), :]
        acc_ref[i] += p_ref[i] * jnp.dot(a_ref[i], vb, preferred_element_type=F32)


def _sb_stage1(z2, upper):
    hi = jnp.maximum(z2, 0.0)
    lo = z2 - hi
    log1p = jnp.log2(1.0 + jnp.exp2(lo - hi))
    sp = hi + log1p
    suffix = jnp.dot(sp.astype(BF16), upper, preferred_element_type=F32)
    return lo - log1p, suffix, suffix[:, 0:1] + sp[:, 0:1]


def _sb_step(qs, k_ref, v_ref, ks, upper, z_ref, a_ref, p_ref, c_ref, acc_ref):
    t = SB_T
    last = v_ref.shape[1] - t
    ks_next = pl.multiple_of(jnp.maximum(ks - t, 0), t)
    ks_prev = pl.multiple_of(jnp.minimum(ks + t, last), t)
    stage1 = []
    for i, q in enumerate(qs):
        z2 = z_ref[i]
        _sb_scores(qs[i:i + 1], k_ref.at[i // 2:i // 2 + 1], ks_next, z_ref.at[i:i + 1])
        _sb_values(v_ref.at[i // 2:i // 2 + 1], ks_prev, a_ref.at[i:i + 1], p_ref.at[i:i + 1],
                   acc_ref.at[i:i + 1])
        stage1.append(_sb_stage1(z2, upper))
    for i, (log_beta, suffix, total) in enumerate(stage1):
        neg_carry = c_ref[i]
        a_ref[i] = jnp.exp2(log_beta - suffix).astype(BF16)
        p_ref[i] = jnp.exp2(neg_carry)
        c_ref[i] = neg_carry - total


def _sb_kernel(q_ref, k_ref, v_ref, o_ref, z_ref, a_ref, p_ref, c_ref, acc_ref):
    t = SB_T
    nb = q_ref.shape[0]
    qi = pl.program_id(1)
    lane = lax.broadcasted_iota(jnp.int32, (t, LANES), 1)
    first = lane < SB_HEAD_DIM
    qs = []
    for bi in range(nb):
        q2 = q_ref[bi]
        zero = jnp.zeros_like(q2)
        qs += [jnp.where(first, q2, zero), jnp.where(first, zero, q2)]
    row = lax.broadcasted_iota(jnp.int32, (t, t), 0)
    col = lax.broadcasted_iota(jnp.int32, (t, t), 1)
    upper = jnp.where(row > col, 1.0, 0.0).astype(BF16)
    causal = col < row

    for ref in (a_ref, p_ref, c_ref, acc_ref):
        ref[...] = jnp.zeros_like(ref)
    diag = pl.multiple_of(qi * t, t)
    for i, q in enumerate(qs):
        z = lax.dot_general(q, k_ref[i // 2, pl.ds(diag, t), :], (((1,), (1,)), ((), ())),
                            preferred_element_type=F32)
        z_ref[i] = jnp.where(causal, z, -jnp.inf)

    @pl.loop(0, qi + 1)
    def _(j):
        ks = pl.multiple_of((qi - j) * t, t)
        _sb_step(qs, k_ref, v_ref, ks, upper, z_ref, a_ref, p_ref, c_ref, acc_ref)

    _sb_values(v_ref, 0, a_ref, p_ref, acc_ref)
    for bi in range(nb):
        o_ref[bi] = jnp.where(first, acc_ref[2 * bi], acc_ref[2 * bi + 1]).astype(o_ref.dtype)


def _sb_attention(proj, b, s):
    t, nb = SB_T, SB_BATCH
    ns = 2 * nb
    qc, kc, vc = (3 * MIX_W) // LANES, (4 * MIX_W) // LANES, (5 * MIX_W) // LANES
    pairs = MIX_W // LANES
    return pl.pallas_call(
        _sb_kernel,
        grid=(b // nb * pairs, s // t),
        in_specs=[
            pl.BlockSpec((nb, t, LANES), lambda g, i: (g // pairs, i, qc + g % pairs)),
            pl.BlockSpec((nb, s, LANES), lambda g, i: (g // pairs, 0, kc + g % pairs)),
            pl.BlockSpec((nb, s, LANES), lambda g, i: (g // pairs, 0, vc + g % pairs)),
        ],
        out_specs=pl.BlockSpec((nb, t, LANES), lambda g, i: (g // pairs, i, g % pairs)),
        out_shape=jax.ShapeDtypeStruct((b, s, MIX_W), BF16),
        scratch_shapes=[pltpu.VMEM((ns, t, t), F32), pltpu.VMEM((ns, t, t), BF16),
                        pltpu.VMEM((ns, t, 1), F32), pltpu.VMEM((ns, t, 1), F32),
                        pltpu.VMEM((ns, t, LANES), F32)],
        compiler_params=_params(("parallel", "arbitrary")),
        name="sb_attention",
    )(proj, proj, proj)


RET_STEP_CHUNKS = 4


def _ret_kernel(q_ref, k_ref, v_ref, g_ref, gain_ref, dintra_ref, kdec_ref, qdec_ref, cdec_ref,
                o_ref, state_ref):
    c, d = RET_CHUNK, RET_HEAD_DIM

    @pl.when(pl.program_id(1) == 0)
    def _():
        state_ref[...] = jnp.zeros_like(state_ref)

    units = [(h, n) for h in range(RET_HEADS) for n in range(RET_STEP_CHUNKS)]
    blk = lambda ref, h, n: ref[0, pl.ds(n * c, c), pl.ds(h * d, d)]
    qb, scores, kv = {}, {}, {}
    for h, n in units:
        qb[h, n] = blk(q_ref, h, n)
        k = blk(k_ref, h, n)
        scores[h, n] = lax.dot_general(qb[h, n], k, (((1,), (1,)), ((), ())),
                                       preferred_element_type=F32)
        kv[h, n] = lax.dot_general((k.astype(F32) * kdec_ref[h]).astype(BF16), blk(v_ref, h, n),
                                   (((0,), (0,)), ((), ())), preferred_element_type=F32)
    inner = {}
    for h, n in units:
        p = (scores[h, n] * dintra_ref[h]).astype(BF16)
        inner[h, n] = jnp.dot(p, blk(v_ref, h, n), preferred_element_type=F32)
    cross = {}
    for h in range(RET_HEADS):
        state = state_ref[h]
        for n in range(RET_STEP_CHUNKS):
            cross[h, n] = jnp.dot(qb[h, n], state.astype(BF16), preferred_element_type=F32)
            state = cdec_ref[h] * state + kv[h, n]
        state_ref[h] = state
    for h, n in units:
        o = inner[h, n] + cross[h, n] * qdec_ref[h]
        mu = jnp.mean(o, axis=-1, keepdims=True)
        oc = o - mu
        var = jnp.mean(oc * oc, axis=-1, keepdims=True)
        on = oc * lax.rsqrt(var + EPS) * gain_ref[:, pl.ds(h * d, d)]
        o_ref[0, pl.ds(n * c, c), pl.ds(h * d, d)] = (
            blk(g_ref, h, n).astype(F32) * on).astype(o_ref.dtype)


def _rotary_tables(s):
    pos = jnp.arange(s, dtype=F32)
    inv_freq = ROPE_BASE ** (-jnp.arange(0, RET_HEAD_DIM, 2, dtype=F32) / RET_HEAD_DIM)
    ang = pos[:, None] * inv_freq[None, :]
    cos, sin = jnp.cos(ang), jnp.sin(ang)
    return jnp.concatenate([cos, cos], axis=-1), jnp.concatenate([-sin, sin], axis=-1)


def _retention_tables():
    log_gamma = jnp.log1p(-jnp.exp2(-5.0 - jnp.arange(RET_HEADS, dtype=F32)))
    idx = jnp.arange(RET_CHUNK, dtype=F32)
    rel = idx[:, None] - idx[None, :]
    dintra = jnp.where(rel >= 0, jnp.exp(jnp.maximum(rel, 0.0)[None] * log_gamma[:, None, None]), 0.0)
    kdec = jnp.exp((RET_CHUNK - 1 - idx)[None] * log_gamma[:, None])
    qdec = jnp.exp((idx + 1)[None] * log_gamma[:, None])
    cdec = jnp.exp(RET_CHUNK * log_gamma)
    bc = lambda a: jnp.broadcast_to(a[:, :, None], (RET_HEADS, RET_CHUNK, RET_HEAD_DIM))
    cdec = jnp.broadcast_to(cdec[:, None, None], (RET_HEADS, 1, RET_HEAD_DIM))
    return dintra, bc(kdec), bc(qdec), cdec


def _retention(proj, gain, b, s):
    ts = RET_STEP_CHUNKS * RET_CHUNK
    c, d = RET_CHUNK, RET_HEAD_DIM
    base = (6 * MIX_W) // MIX_W
    dintra, kdec, qdec, cdec = _retention_tables()
    col = lambda off: pl.BlockSpec((1, ts, MIX_W), lambda bi, i: (bi, i, base + off))
    full = lambda shape: pl.BlockSpec(shape, lambda bi, i: (0,) * len(shape))
    return pl.pallas_call(
        _ret_kernel,
        grid=(b, s // ts),
        in_specs=[col(0), col(1), col(2), col(3), full((1, MIX_W)),
                  full((RET_HEADS, c, c)), full((RET_HEADS, c, d)), full((RET_HEADS, c, d)),
                  full((RET_HEADS, 1, d))],
        out_specs=pl.BlockSpec((1, ts, MIX_W), lambda bi, i: (bi, i, 0)),
        out_shape=jax.ShapeDtypeStruct((b, s, MIX_W), BF16),
        scratch_shapes=[pltpu.VMEM((RET_HEADS, d, d), F32)],
        compiler_params=_params(("parallel", "arbitrary")),
        name="retention",
    )(proj, proj, proj, proj, gain.reshape(1, MIX_W), dintra, kdec, qdec, cdec)


MERGE_TM = 512
HALO = 8


def _merge_kernel(x_ref, cb_ref, cc_ref, ch_ref, ccp_ref, chp_ref, gate0_ref, gate1_ref,
                  gate2_ref, sb_ref, ret_ref, convw_ref, convb_ref, wbr_ref, wo_ref, o_ref):
    tm = MERGE_TM
    gate_refs = (gate0_ref, gate1_ref, gate2_ref)
    u = cc_ref[0].astype(F32) * ch_ref[0].astype(F32)
    prev = ccp_ref[0].astype(F32) * chp_ref[0].astype(F32)
    prev = jnp.where(pl.program_id(1) == 0, 0.0, prev)
    row = lax.broadcasted_iota(jnp.int32, (tm, MIX_W), 0)
    u1 = jnp.where(row == 0, prev[HALO - 1:HALO, :], pltpu.roll(u, 1, 0))
    u2 = jnp.where(row == 0, prev[HALO - 2:HALO - 1, :],
                   jnp.where(row == 1, prev[HALO - 1:HALO, :], pltpu.roll(u, 2, 0)))
    conv = convb_ref[...] + u2 * convw_ref[0:1, :] + u1 * convw_ref[1:2, :] + u * convw_ref[2:3, :]
    conv_out = cb_ref[0].astype(F32) * conv
    branches = (conv_out.astype(BF16), sb_ref[0], ret_ref[0])
    merged = jnp.zeros((tm, D_MODEL), F32)
    for n in range(N_BRANCH):
        up = jnp.dot(branches[n], wbr_ref[n], preferred_element_type=F32)
        merged = merged + gate_refs[n][0].astype(F32) * up
    o_ref[0] = x_ref[0] + jnp.dot(merged.astype(BF16), wo_ref[...], preferred_element_type=F32)


def _merge(x, proj, sb_out, ret_out, conv_w, conv_b, w_branch, w_o):
    b, s, d = x.shape
    tm = MERGE_TM
    w = MIX_W
    gate_blk = (10 * MIX_W) // d
    tok = lambda width, cblk: pl.BlockSpec((1, tm, width), lambda bi, i: (bi, i, cblk))
    halo = lambda cblk: pl.BlockSpec(
        (1, HALO, w), lambda bi, i: (bi, jnp.maximum(i * (tm // HALO) - 1, 0), cblk))
    full = lambda shape: pl.BlockSpec(shape, lambda bi, i: (0,) * len(shape))
    return pl.pallas_call(
        _merge_kernel,
        grid=(b, s // tm),
        in_specs=[tok(d, 0), tok(w, 0), tok(w, 1), tok(w, 2), halo(1), halo(2),
                  tok(d, gate_blk), tok(d, gate_blk + 1), tok(d, gate_blk + 2), tok(w, 0), tok(w, 0),
                  full((CONV_K, w)), full((1, w)), full((N_BRANCH, w, d)), full((d, d))],
        out_specs=tok(d, 0),
        out_shape=jax.ShapeDtypeStruct((b, s, d), F32),
        compiler_params=_params(("parallel", "parallel")),
        name="merge",
    )(x, proj, proj, proj, proj, proj, proj, proj, proj, sb_out, ret_out, conv_w,
      conv_b.reshape(1, w), w_branch, w_o)


XA_TM = 512


def _xattn_kernel(x_ref, g_ref, wq_ref, kv_ref, wo_ref, o_ref):
    x = x_ref[0]
    h = _rmsnorm_f32(x, g_ref[...]).astype(BF16)
    q = jnp.dot(h, wq_ref[...], preferred_element_type=F32)
    q = (q * (XA_HEAD_DIM ** -0.5)).astype(BF16)
    heads = [slice(hd * XA_HEAD_DIM, (hd + 1) * XA_HEAD_DIM) for hd in range(XA_HEADS)]
    scores = [lax.dot_general(q[:, cols], kv_ref[0, :, cols], (((1,), (1,)), ((), ())),
                              preferred_element_type=F32) for cols in heads]
    outs = []
    for cols, sc in zip(heads, scores):
        e = jnp.exp(sc - jnp.max(sc, axis=-1, keepdims=True))
        p = e / jnp.sum(e, axis=-1, keepdims=True)
        vh = kv_ref[0, :, D_MODEL + cols.start:D_MODEL + cols.stop]
        outs.append(jnp.dot(p.astype(BF16), vh, preferred_element_type=F32).astype(BF16))
    o = jnp.concatenate(outs, axis=-1)
    o_ref[0] = x + jnp.dot(o, wo_ref[...], preferred_element_type=F32)


def _xattn(x, g, w_q, kv, w_o):
    b, s, d = x.shape
    tm = XA_TM
    full = lambda shape: pl.BlockSpec(shape, lambda bi, i: (0,) * len(shape))
    return pl.pallas_call(
        _xattn_kernel,
        grid=(b, s // tm),
        in_specs=[pl.BlockSpec((1, tm, d), lambda bi, i: (bi, i, 0)), full((1, d)), full((d, d)),
                  pl.BlockSpec((1, MEM_LEN, 2 * d), lambda bi, i: (bi, 0, 0)), full((d, d))],
        out_specs=pl.BlockSpec((1, tm, d), lambda bi, i: (bi, i, 0)),
        out_shape=jax.ShapeDtypeStruct((b, s, d), F32),
        compiler_params=_params(("parallel", "parallel")),
        name="xattn",
    )(x, g.reshape(1, d), w_q, kv, w_o)


MLP_TM = 1024
MLP_TF = 1024


def _mlp_kernel(x_ref, g_ref, wu_ref, wd_ref, fg_ref, o_ref, h_ref, acc_ref, *, final_norm):
    j = pl.program_id(1)

    rows = [pl.ds(r * (MLP_TM // ROW_SPLIT), MLP_TM // ROW_SPLIT) for r in range(ROW_SPLIT)]

    def step(first):
        if first:
            for r in rows:
                h_ref[r, :] = _rmsnorm_f32(x_ref[r, :], g_ref[...]).astype(BF16)
        ups = [jnp.dot(h_ref[r, :], wu_ref[...], preferred_element_type=F32) for r in rows]
        acts = [jnp.square(jnp.maximum(up, 0.0)).astype(BF16) for up in ups]
        for r, act in zip(rows, acts):
            down = jnp.dot(act, wd_ref[...], preferred_element_type=F32)
            acc_ref[r, :] = (x_ref[r, :] if first else acc_ref[r, :]) + down

    pl.when(j == 0)(functools.partial(step, True))
    pl.when(j > 0)(functools.partial(step, False))

    @pl.when(j == pl.num_programs(1) - 1)
    def _():
        y = acc_ref[...]
        o_ref[...] = _rmsnorm_f32(y, fg_ref[...]) if final_norm else y


def _mlp(x2d, g, w_up, w_down, final_g, final_norm):
    m, d = x2d.shape
    f = w_up.shape[1]
    tm, tf = MLP_TM, MLP_TF
    return pl.pallas_call(
        functools.partial(_mlp_kernel, final_norm=final_norm),
        grid=(m // tm, f // tf),
        in_specs=[pl.BlockSpec((tm, d), lambda i, j: (i, 0)),
                  pl.BlockSpec((1, d), lambda i, j: (0, 0)),
                  pl.BlockSpec((d, tf), lambda i, j: (0, j)),
                  pl.BlockSpec((tf, d), lambda i, j: (j, 0)),
                  pl.BlockSpec((1, d), lambda i, j: (0, 0))],
        out_specs=pl.BlockSpec((tm, d), lambda i, j: (i, 0)),
        out_shape=jax.ShapeDtypeStruct((m, d), F32),
        scratch_shapes=[pltpu.VMEM((tm, d), BF16), pltpu.VMEM((tm, d), F32)],
        compiler_params=_params(("parallel", "arbitrary")),
        name="mlp",
    )(x2d, g.reshape(1, d), w_up, w_down, final_g.reshape(1, d))


def kernel(x, mem, norm_mix_g, w_in, b_gate, conv_w, conv_b, ret_norm_g, w_branch, w_o,
           norm_xa_g, norm_mem_g, w_xq, w_xkv, w_xo, norm_mlp_g, w_up, w_down, final_g):
    b, s, d = x.shape
    depth = w_in.shape[0]
    assert d == D_MODEL and s % SB_T == 0 and s % MERGE_TM == 0 and (b * s) % MLP_TM == 0
    assert b % SB_BATCH == 0
    mem2d = mem.reshape(b * MEM_LEN, d)
    col_scale = jnp.ones((IN_COLS,), F32).at[3 * MIX_W:4 * MIX_W].set(SB_HEAD_DIM ** -0.5 * LOG2E)
    cos2, sin2 = _rotary_tables(s)
    for l in range(depth):
        w_in_l = (w_in[l] * col_scale).astype(BF16)
        proj = _in_proj(x.reshape(b * s, d), norm_mix_g[l], w_in_l, b_gate[l], cos2, sin2, s)
        proj = proj.reshape(b, s, IN_COLS)
        sb_out = _sb_attention(proj, b, s)
        ret_out = _retention(proj, ret_norm_g[l], b, s)
        x = _merge(x, proj, sb_out, ret_out, conv_w[l], conv_b[l], w_branch[l].astype(BF16),
                   w_o[l].astype(BF16))
        kv = _norm_matmul(mem2d, norm_mem_g[l], w_xkv[l].astype(BF16), b * MEM_LEN, 1024)
        x = _xattn(x, norm_xa_g[l], w_xq[l].astype(BF16), kv.reshape(b, MEM_LEN, 2 * d),
                   w_xo[l].astype(BF16))
        x = _mlp(x.reshape(b * s, d), norm_mlp_g[l], w_up[l].astype(BF16), w_down[l].astype(BF16),
                 final_g, l == depth - 1).reshape(b, s, d)
    return x
```

```python
import functools
import math

import jax
import jax.numpy as jnp
from jax import lax
from jax.experimental import pallas as pl
from jax.experimental.pallas import tpu as pltpu

D_MODEL = 1024
MEM_LEN = 256
N_BRANCH = 3
MIX_W = D_MODEL // 2
CONV_K = 3
SB_HEADS = 8
SB_HEAD_DIM = MIX_W // SB_HEADS
RET_HEADS = 4
RET_HEAD_DIM = MIX_W // RET_HEADS
RET_CHUNK = 128
XA_HEADS = 4
XA_HEAD_DIM = D_MODEL // XA_HEADS
D_FF = 4 * D_MODEL
ROPE_BASE = 10000.0
EPS = 1e-6
IN_COLS = 10 * MIX_W + N_BRANCH * D_MODEL

LANES = 128
ROW_SPLIT = 2
VMEM_LIMIT = 48 * 1024 * 1024

F32 = jnp.float32
BF16 = jnp.bfloat16


def _params(semantics):
    return pltpu.CompilerParams(dimension_semantics=semantics, vmem_limit_bytes=VMEM_LIMIT)


def _rmsnorm_f32(xf, g):
    return xf * lax.rsqrt(jnp.mean(xf * xf, axis=-1, keepdims=True) + EPS) * g


def _norm_matmul_kernel(x_ref, g_ref, w_ref, o_ref, h_ref):
    @pl.when(pl.program_id(1) == 0)
    def _():
        h_ref[...] = _rmsnorm_f32(x_ref[...], g_ref[...]).astype(BF16)

    o_ref[...] = jnp.dot(h_ref[...], w_ref[...], preferred_element_type=F32).astype(o_ref.dtype)


def _norm_matmul(x2d, g, w, tm, tn):
    m, d = x2d.shape
    n = w.shape[1]
    return pl.pallas_call(
        _norm_matmul_kernel,
        grid=(m // tm, n // tn),
        in_specs=[
            pl.BlockSpec((tm, d), lambda i, j: (i, 0)),
            pl.BlockSpec((1, d), lambda i, j: (0, 0)),
            pl.BlockSpec((d, tn), lambda i, j: (0, j)),
        ],
        out_specs=pl.BlockSpec((tm, tn), lambda i, j: (i, j)),
        out_shape=jax.ShapeDtypeStruct((m, n), BF16),
        scratch_shapes=[pltpu.VMEM((tm, d), BF16)],
        compiler_params=_params(("parallel", "arbitrary")),
        name="norm_matmul",
    )(x2d, g.reshape(1, d), w)


IN_TM = 1024
IN_TN = 4 * MIX_W
assert IN_COLS == 4 * IN_TN


def _rotary(tf, cos2, sin2):
    return tf * cos2 + pltpu.roll(tf, RET_HEAD_DIM // 2, 1) * sin2


def _in_proj_kernel(x_ref, g_ref, w_ref, bias_ref, cos_ref, sin_ref, o_ref, h_ref):
    j = pl.program_id(1)
    w = MIX_W
    groups = [pl.ds(r * (IN_TM // ROW_SPLIT), IN_TM // ROW_SPLIT) for r in range(ROW_SPLIT)]

    def project():
        return [(rows, jnp.dot(h_ref[rows, :], w_ref[...], preferred_element_type=F32))
                for rows in groups]

    @pl.when(j == 0)
    def _():
        for rows in groups:
            h_ref[rows, :] = _rmsnorm_f32(x_ref[rows, :], g_ref[...]).astype(BF16)
        for rows, acc in project():
            o_ref[rows, :] = acc.astype(BF16)

    @pl.when(j == 1)
    def _():
        for rows, acc in project():
            o_ref[rows, :2 * w] = acc[:, :2 * w].astype(BF16)
            cos2, sin2 = cos_ref[rows, :], sin_ref[rows, :]
            for start, scale in ((2 * w, None), (3 * w, RET_HEAD_DIM ** -0.5)):
                for h in range(RET_HEADS):
                    cols = slice(start + h * RET_HEAD_DIM, start + (h + 1) * RET_HEAD_DIM)
                    r = _rotary(acc[:, cols], cos2, sin2)
                    o_ref[rows, cols] = (r if scale is None else r * scale).astype(BF16)

    @pl.when(j == 2)
    def _():
        for rows, acc in project():
            o_ref[rows, :w] = acc[:, :w].astype(BF16)
            gate = acc[:, w:2 * w]
            o_ref[rows, w:2 * w] = (gate * jax.nn.sigmoid(gate)).astype(BF16)
            o_ref[rows, 2 * w:] = jax.nn.sigmoid(acc[:, 2 * w:] + bias_ref[:, 2 * w:]).astype(BF16)

    @pl.when(j == 3)
    def _():
        for rows, acc in project():
            o_ref[rows, :] = jax.nn.sigmoid(acc + bias_ref[...]).astype(BF16)


def _in_proj(x2d, g, w, b_gate, cos2, sin2, s):
    m, d = x2d.shape
    tm, tn = IN_TM, IN_TN
    bias = jnp.concatenate([jnp.zeros((IN_COLS - N_BRANCH * D_MODEL,), F32), b_gate.reshape(-1)])
    seq_blocks = s // tm
    return pl.pallas_call(
        _in_proj_kernel,
        grid=(m // tm, IN_COLS // tn),
        in_specs=[
            pl.BlockSpec((tm, d), lambda i, j: (i, 0)),
            pl.BlockSpec((1, d), lambda i, j: (0, 0)),
            pl.BlockSpec((d, tn), lambda i, j: (0, j)),
            pl.BlockSpec((1, tn), lambda i, j: (0, j)),
            pl.BlockSpec((tm, RET_HEAD_DIM), lambda i, j: (i % seq_blocks, 0)),
            pl.BlockSpec((tm, RET_HEAD_DIM), lambda i, j: (i % seq_blocks, 0)),
        ],
        out_specs=pl.BlockSpec((tm, tn), lambda i, j: (i, j)),
        out_shape=jax.ShapeDtypeStruct((m, IN_COLS), BF16),
        scratch_shapes=[pltpu.VMEM((tm, d), BF16)],
        compiler_params=_params(("parallel", "arbitrary")),
        name="in_proj",
    )(x2d, g.reshape(1, d), w, bias.reshape(1, IN_COLS), cos2, sin2)


SB_T = 256
SB_BATCH = 4
LOG2E = math.log2(math.e)
LOG2E_HI = 1.4375
LOG2E_LO = LOG2E - LOG2E_HI


def _sb_scores(qs, k_ref, ks, z_ref):
    for i, q in enumerate(qs):
        kb = k_ref[i // 2, pl.ds(ks, SB_T), :]
        z_ref[i] = lax.dot_general(q, kb, (((1,), (1,)), ((), ())), preferred_element_type=F32)


def _sb_values(v_ref, ks, a_ref, p_ref, acc_ref):
    for i in range(a_ref.shape[0]):
        vb = v_ref[i // 2, pl.ds(ks, SB_T), :]
        acc_ref[i] += p_ref[i] * jnp.dot(a_ref[i], vb, preferred_element_type=F32)


def _sb_stage1(z2, upper):
    zb = z2.astype(BF16)
    hi = jnp.maximum(zb, 0.0)
    lo = zb - hi
    ln1p = jnp.log(1.0 + jnp.exp2(lo - hi))
    log1p = ln1p * LOG2E_HI + ln1p * LOG2E_LO
    sp = hi + log1p
    suffix = jnp.dot(sp, upper, preferred_element_type=F32)
    return (lo - log1p).astype(F32), suffix, suffix[:, 0:1] + sp[:, 0:1].astype(F32)


def _sb_step(qs, k_ref, v_ref, ks, upper, z_ref, a_ref, p_ref, c_ref, acc_ref, first):
    t = SB_T
    ks_next = pl.multiple_of(jnp.maximum(ks - t, 0), t)
    stage1 = []
    for i, q in enumerate(qs):
        z2 = z_ref[i]
        _sb_scores(qs[i:i + 1], k_ref.at[i // 2:i // 2 + 1], ks_next, z_ref.at[i:i + 1])
        if not first:
            _sb_values(v_ref.at[i // 2:i // 2 + 1], pl.multiple_of(ks + t, t), a_ref.at[i:i + 1],
                       p_ref.at[i:i + 1], acc_ref.at[i:i + 1])
        stage1.append(_sb_stage1(z2, upper))
    for i, (log_beta, suffix, total) in enumerate(stage1):
        neg_carry = c_ref[i]
        a_ref[i] = jnp.exp2(log_beta - suffix).astype(BF16)
        p_ref[i] = jnp.exp2(neg_carry)
        c_ref[i] = neg_carry - total


def _sb_kernel(q_ref, k_ref, v_ref, o_ref, z_ref, a_ref, p_ref, c_ref, acc_ref):
    t = SB_T
    nb = q_ref.shape[0]
    qi = pl.program_id(1)
    lane = lax.broadcasted_iota(jnp.int32, (t, LANES), 1)
    first = lane < SB_HEAD_DIM
    qs = []
    for bi in range(nb):
        q2 = q_ref[bi]
        zero = jnp.zeros_like(q2)
        qs += [jnp.where(first, q2, zero), jnp.where(first, zero, q2)]
    row = lax.broadcasted_iota(jnp.int32, (t, t), 0)
    col = lax.broadcasted_iota(jnp.int32, (t, t), 1)
    upper = jnp.where(row > col, 1.0, 0.0).astype(BF16)
    causal = col < row

    c_ref[...] = jnp.zeros_like(c_ref)
    acc_ref[...] = jnp.zeros_like(acc_ref)
    diag = pl.multiple_of(qi * t, t)
    for i, q in enumerate(qs):
        z = lax.dot_general(q, k_ref[i // 2, pl.ds(diag, t), :], (((1,), (1,)), ((), ())),
                            preferred_element_type=F32)
        z_ref[i] = jnp.where(causal, z, -jnp.inf)
    _sb_step(qs, k_ref, v_ref, diag, upper, z_ref, a_ref, p_ref, c_ref, acc_ref, True)

    @pl.loop(0, qi)
    def _(j):
        ks = pl.multiple_of((qi - 1 - j) * t, t)
        _sb_step(qs, k_ref, v_ref, ks, upper, z_ref, a_ref, p_ref, c_ref, acc_ref, False)

    _sb_values(v_ref, 0, a_ref, p_ref, acc_ref)
    for bi in range(nb):
        o_ref[bi] = jnp.where(first, acc_ref[2 * bi], acc_ref[2 * bi + 1]).astype(o_ref.dtype)


def _sb_attention(proj, b, s):
    t, nb = SB_T, SB_BATCH
    ns = 2 * nb
    qc, kc, vc = (3 * MIX_W) // LANES, (4 * MIX_W) // LANES, (5 * MIX_W) // LANES
    pairs = MIX_W // LANES
    return pl.pallas_call(
        _sb_kernel,
        grid=(b // nb * pairs, s // t),
        in_specs=[
            pl.BlockSpec((nb, t, LANES), lambda g, i: (g // pairs, i, qc + g % pairs)),
            pl.BlockSpec((nb, s, LANES), lambda g, i: (g // pairs, 0, kc + g % pairs)),
            pl.BlockSpec((nb, s, LANES), lambda g, i: (g // pairs, 0, vc + g % pairs)),
        ],
        out_specs=pl.BlockSpec((nb, t, LANES), lambda g, i: (g // pairs, i, g % pairs)),
        out_shape=jax.ShapeDtypeStruct((b, s, MIX_W), BF16),
        scratch_shapes=[pltpu.VMEM((ns, t, t), F32), pltpu.VMEM((ns, t, t), BF16),
                        pltpu.VMEM((ns, t, 1), F32), pltpu.VMEM((ns, t, 1), F32),
                        pltpu.VMEM((ns, t, LANES), F32)],
        compiler_params=_params(("parallel", "arbitrary")),
        name="sb_attention",
    )(proj, proj, proj)


RET_STEP_CHUNKS = 4


def _ret_kernel(q_ref, k_ref, v_ref, g_ref, gain_ref, dintra_ref, kdec_ref, qdec_ref, cdec_ref,
                o_ref, state_ref):
    c, d = RET_CHUNK, RET_HEAD_DIM

    @pl.when(pl.program_id(1) == 0)
    def _():
        state_ref[...] = jnp.zeros_like(state_ref)

    units = [(h, n) for h in range(RET_HEADS) for n in range(RET_STEP_CHUNKS)]
    blk = lambda ref, h, n: ref[0, pl.ds(n * c, c), pl.ds(h * d, d)]
    qb, scores, kv = {}, {}, {}
    for h, n in units:
        qb[h, n] = blk(q_ref, h, n)
        k = blk(k_ref, h, n)
        scores[h, n] = lax.dot_general(qb[h, n], k, (((1,), (1,)), ((), ())),
                                       preferred_element_type=F32)
        kv[h, n] = lax.dot_general((k.astype(F32) * kdec_ref[h]).astype(BF16), blk(v_ref, h, n),
                                   (((0,), (0,)), ((), ())), preferred_element_type=F32)
    inner = {}
    for h, n in units:
        p = (scores[h, n] * dintra_ref[h]).astype(BF16)
        inner[h, n] = jnp.dot(p, blk(v_ref, h, n), preferred_element_type=F32)
    cross = {}
    for h in range(RET_HEADS):
        state = state_ref[h]
        for n in range(RET_STEP_CHUNKS):
            cross[h, n] = jnp.dot(qb[h, n], state.astype(BF16), preferred_element_type=F32)
            state = cdec_ref[h] * state + kv[h, n]
        state_ref[h] = state
    for h, n in units:
        o = inner[h, n] + cross[h, n] * qdec_ref[h]
        mu = jnp.mean(o, axis=-1, keepdims=True)
        oc = o - mu
        var = jnp.mean(oc * oc, axis=-1, keepdims=True)
        on = oc * lax.rsqrt(var + EPS) * gain_ref[:, pl.ds(h * d, d)]
        o_ref[0, pl.ds(n * c, c), pl.ds(h * d, d)] = (
            blk(g_ref, h, n).astype(F32) * on).astype(o_ref.dtype)


def _rotary_tables(s):
    pos = jnp.arange(s, dtype=F32)
    inv_freq = ROPE_BASE ** (-jnp.arange(0, RET_HEAD_DIM, 2, dtype=F32) / RET_HEAD_DIM)
    ang = pos[:, None] * inv_freq[None, :]
    cos, sin = jnp.cos(ang), jnp.sin(ang)
    return jnp.concatenate([cos, cos], axis=-1), jnp.concatenate([-sin, sin], axis=-1)


def _retention_tables():
    log_gamma = jnp.log1p(-jnp.exp2(-5.0 - jnp.arange(RET_HEADS, dtype=F32)))
    idx = jnp.arange(RET_CHUNK, dtype=F32)
    rel = idx[:, None] - idx[None, :]
    dintra = jnp.where(rel >= 0, jnp.exp(jnp.maximum(rel, 0.0)[None] * log_gamma[:, None, None]), 0.0)
    kdec = jnp.exp((RET_CHUNK - 1 - idx)[None] * log_gamma[:, None])
    qdec = jnp.exp((idx + 1)[None] * log_gamma[:, None])
    cdec = jnp.exp(RET_CHUNK * log_gamma)
    bc = lambda a: jnp.broadcast_to(a[:, :, None], (RET_HEADS, RET_CHUNK, RET_HEAD_DIM))
    cdec = jnp.broadcast_to(cdec[:, None, None], (RET_HEADS, 1, RET_HEAD_DIM))
    return dintra, bc(kdec), bc(qdec), cdec


def _retention(proj, gain, b, s):
    ts = RET_STEP_CHUNKS * RET_CHUNK
    c, d = RET_CHUNK, RET_HEAD_DIM
    base = (6 * MIX_W) // MIX_W
    dintra, kdec, qdec, cdec = _retention_tables()
    col = lambda off: pl.BlockSpec((1, ts, MIX_W), lambda bi, i: (bi, i, base + off))
    full = lambda shape: pl.BlockSpec(shape, lambda bi, i: (0,) * len(shape))
    return pl.pallas_call(
        _ret_kernel,
        grid=(b, s // ts),
        in_specs=[col(0), col(1), col(2), col(3), full((1, MIX_W)),
                  full((RET_HEADS, c, c)), full((RET_HEADS, c, d)), full((RET_HEADS, c, d)),
                  full((RET_HEADS, 1, d))],
        out_specs=pl.BlockSpec((1, ts, MIX_W), lambda bi, i: (bi, i, 0)),
        out_shape=jax.ShapeDtypeStruct((b, s, MIX_W), BF16),
        scratch_shapes=[pltpu.VMEM((RET_HEADS, d, d), F32)],
        compiler_params=_params(("parallel", "arbitrary")),
        name="retention",
    )(proj, proj, proj, proj, gain.reshape(1, MIX_W), dintra, kdec, qdec, cdec)


MERGE_TM = 512
HALO = 8


def _merge_kernel(x_ref, cb_ref, cc_ref, ch_ref, ccp_ref, chp_ref, gate0_ref, gate1_ref,
                  gate2_ref, sb_ref, ret_ref, convw_ref, convb_ref, wbr_ref, wo_ref, o_ref):
    tm = MERGE_TM
    gate_refs = (gate0_ref, gate1_ref, gate2_ref)
    u = cc_ref[0].astype(F32) * ch_ref[0].astype(F32)
    prev = ccp_ref[0].astype(F32) * chp_ref[0].astype(F32)
    prev = jnp.where(pl.program_id(1) == 0, 0.0, prev)
    row = lax.broadcasted_iota(jnp.int32, (tm, MIX_W), 0)
    u1 = jnp.where(row == 0, prev[HALO - 1:HALO, :], pltpu.roll(u, 1, 0))
    u2 = jnp.where(row == 0, prev[HALO - 2:HALO - 1, :],
                   jnp.where(row == 1, prev[HALO - 1:HALO, :], pltpu.roll(u, 2, 0)))
    conv = convb_ref[...] + u2 * convw_ref[0:1, :] + u1 * convw_ref[1:2, :] + u * convw_ref[2:3, :]
    conv_out = cb_ref[0].astype(F32) * conv
    branches = (conv_out.astype(BF16), sb_ref[0], ret_ref[0])
    merged = jnp.zeros((tm, D_MODEL), F32)
    for n in range(N_BRANCH):
        up = jnp.dot(branches[n], wbr_ref[n], preferred_element_type=F32)
        merged = merged + gate_refs[n][0].astype(F32) * up
    o_ref[0] = x_ref[0] + jnp.dot(merged.astype(BF16), wo_ref[...], preferred_element_type=F32)


def _merge(x, proj, sb_out, ret_out, conv_w, conv_b, w_branch, w_o):
    b, s, d = x.shape
    tm = MERGE_TM
    w = MIX_W
    gate_blk = (10 * MIX_W) // d
    tok = lambda width, cblk: pl.BlockSpec((1, tm, width), lambda bi, i: (bi, i, cblk))
    halo = lambda cblk: pl.BlockSpec(
        (1, HALO, w), lambda bi, i: (bi, jnp.maximum(i * (tm // HALO) - 1, 0), cblk))
    full = lambda shape: pl.BlockSpec(shape, lambda bi, i: (0,) * len(shape))
    return pl.pallas_call(
        _merge_kernel,
        grid=(b, s // tm),
        in_specs=[tok(d, 0), tok(w, 0), tok(w, 1), tok(w, 2), halo(1), halo(2),
                  tok(d, gate_blk), tok(d, gate_blk + 1), tok(d, gate_blk + 2), tok(w, 0), tok(w, 0),
                  full((CONV_K, w)), full((1, w)), full((N_BRANCH, w, d)), full((d, d))],
        out_specs=tok(d, 0),
        out_shape=jax.ShapeDtypeStruct((b, s, d), F32),
        compiler_params=_params(("parallel", "parallel")),
        name="merge",
    )(x, proj, proj, proj, proj, proj, proj, proj, proj, sb_out, ret_out, conv_w,
      conv_b.reshape(1, w), w_branch, w_o)


XA_TM = 512


def _xattn_kernel(x_ref, g_ref, wq_ref, kv_ref, wo_ref, o_ref):
    x = x_ref[0]
    h = _rmsnorm_f32(x, g_ref[...]).astype(BF16)
    q = jnp.dot(h, wq_ref[...], preferred_element_type=F32)
    q = (q * (XA_HEAD_DIM ** -0.5)).astype(BF16)
    heads = [slice(hd * XA_HEAD_DIM, (hd + 1) * XA_HEAD_DIM) for hd in range(XA_HEADS)]
    scores = [lax.dot_general(q[:, cols], kv_ref[0, :, cols], (((1,), (1,)), ((), ())),
                              preferred_element_type=F32) for cols in heads]
    outs = []
    for cols, sc in zip(heads, scores):
        e = jnp.exp(sc - jnp.max(sc, axis=-1, keepdims=True))
        p = e / jnp.sum(e, axis=-1, keepdims=True)
        vh = kv_ref[0, :, D_MODEL + cols.start:D_MODEL + cols.stop]
        outs.append(jnp.dot(p.astype(BF16), vh, preferred_element_type=F32).astype(BF16))
    o = jnp.concatenate(outs, axis=-1)
    o_ref[0] = x + jnp.dot(o, wo_ref[...], preferred_element_type=F32)


def _xattn(x, g, w_q, kv, w_o):
    b, s, d = x.shape
    tm = XA_TM
    full = lambda shape: pl.BlockSpec(shape, lambda bi, i: (0,) * len(shape))
    return pl.pallas_call(
        _xattn_kernel,
        grid=(b, s // tm),
        in_specs=[pl.BlockSpec((1, tm, d), lambda bi, i: (bi, i, 0)), full((1, d)), full((d, d)),
                  pl.BlockSpec((1, MEM_LEN, 2 * d), lambda bi, i: (bi, 0, 0)), full((d, d))],
        out_specs=pl.BlockSpec((1, tm, d), lambda bi, i: (bi, i, 0)),
        out_shape=jax.ShapeDtypeStruct((b, s, d), F32),
        compiler_params=_params(("parallel", "parallel")),
        name="xattn",
    )(x, g.reshape(1, d), w_q, kv, w_o)


MLP_TM = 1024
MLP_TF = 1024


def _mlp_kernel(x_ref, g_ref, wu_ref, wd_ref, fg_ref, o_ref, h_ref, acc_ref, *, final_norm):
    j = pl.program_id(1)

    rows = [pl.ds(r * (MLP_TM // ROW_SPLIT), MLP_TM // ROW_SPLIT) for r in range(ROW_SPLIT)]

    def step(first):
        if first:
            for r in rows:
                h_ref[r, :] = _rmsnorm_f32(x_ref[r, :], g_ref[...]).astype(BF16)
        ups = [jnp.dot(h_ref[r, :], wu_ref[...], preferred_element_type=F32) for r in rows]
        acts = [jnp.square(jnp.maximum(up, 0.0)).astype(BF16) for up in ups]
        for r, act in zip(rows, acts):
            down = jnp.dot(act, wd_ref[...], preferred_element_type=F32)
            acc_ref[r, :] = (x_ref[r, :] if first else acc_ref[r, :]) + down

    pl.when(j == 0)(functools.partial(step, True))
    pl.when(j > 0)(functools.partial(step, False))

    @pl.when(j == pl.num_programs(1) - 1)
    def _():
        y = acc_ref[...]
        o_ref[...] = _rmsnorm_f32(y, fg_ref[...]) if final_norm else y


def _mlp(x2d, g, w_up, w_down, final_g, final_norm):
    m, d = x2d.shape
    f = w_up.shape[1]
    tm, tf = MLP_TM, MLP_TF
    return pl.pallas_call(
        functools.partial(_mlp_kernel, final_norm=final_norm),
        grid=(m // tm, f // tf),
        in_specs=[pl.BlockSpec((tm, d), lambda i, j: (i, 0)),
                  pl.BlockSpec((1, d), lambda i, j: (0, 0)),
                  pl.BlockSpec((d, tf), lambda i, j: (0, j)),
                  pl.BlockSpec((tf, d), lambda i, j: (j, 0)),
                  pl.BlockSpec((1, d), lambda i, j: (0, 0))],
        out_specs=pl.BlockSpec((tm, d), lambda i, j: (i, 0)),
        out_shape=jax.ShapeDtypeStruct((m, d), F32),
        scratch_shapes=[pltpu.VMEM((tm, d), BF16), pltpu.VMEM((tm, d), F32)],
        compiler_params=_params(("parallel", "arbitrary")),
        name="mlp",
    )(x2d, g.reshape(1, d), w_up, w_down, final_g.reshape(1, d))


def kernel(x, mem, norm_mix_g, w_in, b_gate, conv_w, conv_b, ret_norm_g, w_branch, w_o,
           norm_xa_g, norm_mem_g, w_xq, w_xkv, w_xo, norm_mlp_g, w_up, w_down, final_g):
    b, s, d = x.shape
    depth = w_in.shape[0]
    assert d == D_MODEL and s % SB_T == 0 and s % MERGE_TM == 0 and (b * s) % MLP_TM == 0
    assert b % SB_BATCH == 0
    mem2d = mem.reshape(b * MEM_LEN, d)
    col_scale = jnp.ones((IN_COLS,), F32).at[3 * MIX_W:4 * MIX_W].set(SB_HEAD_DIM ** -0.5 * LOG2E)
    cos2, sin2 = _rotary_tables(s)
    for l in range(depth):
        w_in_l = (w_in[l] * col_scale).astype(BF16)
        proj = _in_proj(x.reshape(b * s, d), norm_mix_g[l], w_in_l, b_gate[l], cos2, sin2, s)
        proj = proj.reshape(b, s, IN_COLS)
        sb_out = _sb_attention(proj, b, s)
        ret_out = _retention(proj, ret_norm_g[l], b, s)
        x = _merge(x, proj, sb_out, ret_out, conv_w[l], conv_b[l], w_branch[l].astype(BF16),
                   w_o[l].astype(BF16))
        kv = _norm_matmul(mem2d, norm_mem_g[l], w_xkv[l].astype(BF16), b * MEM_LEN, 1024)
        x = _xattn(x, norm_xa_g[l], w_xq[l].astype(BF16), kv.reshape(b, MEM_LEN, 2 * d),
                   w_xo[l].astype(BF16))
        x = _mlp(x.reshape(b * s, d), norm_mlp_g[l], w_up[l].astype(BF16), w_down[l].astype(BF16),
                 final_g, l == depth - 1).reshape(b, s, d)
    return x
```

```python
import functools
import math

import jax
import jax.numpy as jnp
from jax import lax
from jax.experimental import pallas as pl
from jax.experimental.pallas import tpu as pltpu

D_MODEL = 1024
MEM_LEN = 256
N_BRANCH = 3
MIX_W = D_MODEL // 2
CONV_K = 3
SB_HEADS = 8
SB_HEAD_DIM = MIX_W // SB_HEADS
RET_HEADS = 4
RET_HEAD_DIM = MIX_W // RET_HEADS
RET_CHUNK = 128
XA_HEADS = 4
XA_HEAD_DIM = D_MODEL // XA_HEADS
D_FF = 4 * D_MODEL
ROPE_BASE = 10000.0
EPS = 1e-6
IN_COLS = 10 * MIX_W + N_BRANCH * D_MODEL

LANES = 128
ROW_SPLIT = 4
VMEM_LIMIT = 48 * 1024 * 1024

F32 = jnp.float32
BF16 = jnp.bfloat16


def _params(semantics):
    return pltpu.CompilerParams(dimension_semantics=semantics, vmem_limit_bytes=VMEM_LIMIT)


def _rmsnorm_f32(xf, g):
    return xf * lax.rsqrt(jnp.mean(xf * xf, axis=-1, keepdims=True) + EPS) * g


def _norm_matmul_kernel(x_ref, g_ref, w_ref, o_ref, h_ref):
    @pl.when(pl.program_id(1) == 0)
    def _():
        h_ref[...] = _rmsnorm_f32(x_ref[...], g_ref[...]).astype(BF16)

    o_ref[...] = jnp.dot(h_ref[...], w_ref[...], preferred_element_type=F32).astype(o_ref.dtype)


def _norm_matmul(x2d, g, w, tm, tn):
    m, d = x2d.shape
    n = w.shape[1]
    return pl.pallas_call(
        _norm_matmul_kernel,
        grid=(m // tm, n // tn),
        in_specs=[
            pl.BlockSpec((tm, d), lambda i, j: (i, 0)),
            pl.BlockSpec((1, d), lambda i, j: (0, 0)),
            pl.BlockSpec((d, tn), lambda i, j: (0, j)),
        ],
        out_specs=pl.BlockSpec((tm, tn), lambda i, j: (i, j)),
        out_shape=jax.ShapeDtypeStruct((m, n), BF16),
        scratch_shapes=[pltpu.VMEM((tm, d), BF16)],
        compiler_params=_params(("parallel", "arbitrary")),
        name="norm_matmul",
    )(x2d, g.reshape(1, d), w)


IN_TM = 1024
IN_TN = 4 * MIX_W
assert IN_COLS == 4 * IN_TN


def _rotary(tf, cos2, sin2):
    return tf * cos2 + pltpu.roll(tf, RET_HEAD_DIM // 2, 1) * sin2


def _in_proj_kernel(x_ref, g_ref, w_ref, bias_ref, cos_ref, sin_ref, o_ref, h_ref):
    j = pl.program_id(1)
    w = MIX_W
    groups = [pl.ds(r * (IN_TM // ROW_SPLIT), IN_TM // ROW_SPLIT) for r in range(ROW_SPLIT)]

    def project():
        return [(rows, jnp.dot(h_ref[rows, :], w_ref[...], preferred_element_type=F32))
                for rows in groups]

    @pl.when(j == 0)
    def _():
        for rows in groups:
            h_ref[rows, :] = _rmsnorm_f32(x_ref[rows, :], g_ref[...]).astype(BF16)
        for rows, acc in project():
            o_ref[rows, :] = acc.astype(BF16)

    @pl.when(j == 1)
    def _():
        for rows, acc in project():
            o_ref[rows, :2 * w] = acc[:, :2 * w].astype(BF16)
            cos2, sin2 = cos_ref[rows, :], sin_ref[rows, :]
            for start, scale in ((2 * w, None), (3 * w, RET_HEAD_DIM ** -0.5)):
                for h in range(RET_HEADS):
                    cols = slice(start + h * RET_HEAD_DIM, start + (h + 1) * RET_HEAD_DIM)
                    r = _rotary(acc[:, cols], cos2, sin2)
                    o_ref[rows, cols] = (r if scale is None else r * scale).astype(BF16)

    @pl.when(j == 2)
    def _():
        for rows, acc in project():
            o_ref[rows, :w] = acc[:, :w].astype(BF16)
            gate = acc[:, w:2 * w]
            o_ref[rows, w:2 * w] = (gate * jax.nn.sigmoid(gate)).astype(BF16)
            o_ref[rows, 2 * w:] = jax.nn.sigmoid(acc[:, 2 * w:] + bias_ref[:, 2 * w:]).astype(BF16)

    @pl.when(j == 3)
    def _():
        for rows, acc in project():
            o_ref[rows, :] = jax.nn.sigmoid(acc + bias_ref[...]).astype(BF16)


def _in_proj(x2d, g, w, b_gate, cos2, sin2, s):
    m, d = x2d.shape
    tm, tn = IN_TM, IN_TN
    bias = jnp.concatenate([jnp.zeros((IN_COLS - N_BRANCH * D_MODEL,), F32), b_gate.reshape(-1)])
    seq_blocks = s // tm
    return pl.pallas_call(
        _in_proj_kernel,
        grid=(m // tm, IN_COLS // tn),
        in_specs=[
            pl.BlockSpec((tm, d), lambda i, j: (i, 0)),
            pl.BlockSpec((1, d), lambda i, j: (0, 0)),
            pl.BlockSpec((d, tn), lambda i, j: (0, j)),
            pl.BlockSpec((1, tn), lambda i, j: (0, j)),
            pl.BlockSpec((tm, RET_HEAD_DIM), lambda i, j: (i % seq_blocks, 0)),
            pl.BlockSpec((tm, RET_HEAD_DIM), lambda i, j: (i % seq_blocks, 0)),
        ],
        out_specs=pl.BlockSpec((tm, tn), lambda i, j: (i, j)),
        out_shape=jax.ShapeDtypeStruct((m, IN_COLS), BF16),
        scratch_shapes=[pltpu.VMEM((tm, d), BF16)],
        compiler_params=_params(("parallel", "arbitrary")),
        name="in_proj",
    )(x2d, g.reshape(1, d), w, bias.reshape(1, IN_COLS), cos2, sin2)


SB_T = 256
SB_BATCH = 4
LOG2E = math.log2(math.e)
LOG2E_HI = 1.4375
LOG2E_LO = LOG2E - LOG2E_HI


def _sb_scores(qs, k_ref, ks, z_ref):
    for i, q in enumerate(qs):
        kb = k_ref[i // 2, pl.ds(ks, SB_T), :]
        z_ref[i] = lax.dot_general(q, kb, (((1,), (1,)), ((), ())), preferred_element_type=F32)


def _sb_values(v_ref, ks, a_ref, p_ref, acc_ref):
    for i in range(a_ref.shape[0]):
        vb = v_ref[i // 2, pl.ds(ks, SB_T), :]
        acc_ref[i] += p_ref[i] * jnp.dot(a_ref[i], vb, preferred_element_type=F32)


def _sb_stage1(z2, upper):
    zb = z2.astype(BF16)
    hi = jnp.maximum(zb, 0.0)
    lo = zb - hi
    ln1p = jnp.log(1.0 + jnp.exp2(lo - hi))
    log1p = ln1p * LOG2E_HI + ln1p * LOG2E_LO
    sp = hi + log1p
    suffix = jnp.dot(sp, upper, preferred_element_type=F32)
    return (lo - log1p).astype(F32), suffix, suffix[:, 0:1] + sp[:, 0:1].astype(F32)


def _sb_step(qs, k_ref, v_ref, ks, upper, z_ref, a_ref, p_ref, c_ref, acc_ref, first):
    t = SB_T
    ks_next = pl.multiple_of(jnp.maximum(ks - t, 0), t)
    stage1 = []
    for i, q in enumerate(qs):
        z2 = z_ref[i]
        _sb_scores(qs[i:i + 1], k_ref.at[i // 2:i // 2 + 1], ks_next, z_ref.at[i:i + 1])
        if not first:
            _sb_values(v_ref.at[i // 2:i // 2 + 1], pl.multiple_of(ks + t, t), a_ref.at[i:i + 1],
                       p_ref.at[i:i + 1], acc_ref.at[i:i + 1])
        stage1.append(_sb_stage1(z2, upper))
    for i, (log_beta, suffix, total) in enumerate(stage1):
        neg_carry = c_ref[i]
        a_ref[i] = jnp.exp2(log_beta - suffix).astype(BF16)
        p_ref[i] = jnp.exp2(neg_carry)
        c_ref[i] = neg_carry - total


def _sb_kernel(q_ref, k_ref, v_ref, o_ref, z_ref, a_ref, p_ref, c_ref, acc_ref):
    t = SB_T
    nb = q_ref.shape[0]
    qi = pl.program_id(1)
    lane = lax.broadcasted_iota(jnp.int32, (t, LANES), 1)
    first = lane < SB_HEAD_DIM
    qs = []
    for bi in range(nb):
        q2 = q_ref[bi]
        zero = jnp.zeros_like(q2)
        qs += [jnp.where(first, q2, zero), jnp.where(first, zero, q2)]
    row = lax.broadcasted_iota(jnp.int32, (t, t), 0)
    col = lax.broadcasted_iota(jnp.int32, (t, t), 1)
    upper = jnp.where(row > col, 1.0, 0.0).astype(BF16)
    causal = col < row

    c_ref[...] = jnp.zeros_like(c_ref)
    acc_ref[...] = jnp.zeros_like(acc_ref)
    diag = pl.multiple_of(qi * t, t)
    for i, q in enumerate(qs):
        z = lax.dot_general(q, k_ref[i // 2, pl.ds(diag, t), :], (((1,), (1,)), ((), ())),
                            preferred_element_type=F32)
        z_ref[i] = jnp.where(causal, z, -jnp.inf)
    _sb_step(qs, k_ref, v_ref, diag, upper, z_ref, a_ref, p_ref, c_ref, acc_ref, True)

    @pl.loop(0, qi)
    def _(j):
        ks = pl.multiple_of((qi - 1 - j) * t, t)
        _sb_step(qs, k_ref, v_ref, ks, upper, z_ref, a_ref, p_ref, c_ref, acc_ref, False)

    _sb_values(v_ref, 0, a_ref, p_ref, acc_ref)
    for bi in range(nb):
        o_ref[bi] = jnp.where(first, acc_ref[2 * bi], acc_ref[2 * bi + 1]).astype(o_ref.dtype)


def _sb_attention(proj, b, s):
    t, nb = SB_T, SB_BATCH
    ns = 2 * nb
    qc, kc, vc = (3 * MIX_W) // LANES, (4 * MIX_W) // LANES, (5 * MIX_W) // LANES
    pairs = MIX_W // LANES
    return pl.pallas_call(
        _sb_kernel,
        grid=(b // nb * pairs, s // t),
        in_specs=[
            pl.BlockSpec((nb, t, LANES), lambda g, i: (g // pairs, i, qc + g % pairs)),
            pl.BlockSpec((nb, s, LANES), lambda g, i: (g // pairs, 0, kc + g % pairs)),
            pl.BlockSpec((nb, s, LANES), lambda g, i: (g // pairs, 0, vc + g % pairs)),
        ],
        out_specs=pl.BlockSpec((nb, t, LANES), lambda g, i: (g // pairs, i, g % pairs)),
        out_shape=jax.ShapeDtypeStruct((b, s, MIX_W), BF16),
        scratch_shapes=[pltpu.VMEM((ns, t, t), F32), pltpu.VMEM((ns, t, t), BF16),
                        pltpu.VMEM((ns, t, 1), F32), pltpu.VMEM((ns, t, 1), F32),
                        pltpu.VMEM((ns, t, LANES), F32)],
        compiler_params=_params(("parallel", "arbitrary")),
        name="sb_attention",
    )(proj, proj, proj)


RET_STEP_CHUNKS = 8


def _ret_kernel(q_ref, k_ref, v_ref, g_ref, gain_ref, dintra_ref, kdec_ref, qdec_ref, cdec_ref,
                o_ref, state_ref):
    c, d = RET_CHUNK, RET_HEAD_DIM

    @pl.when(pl.program_id(1) == 0)
    def _():
        state_ref[...] = jnp.zeros_like(state_ref)

    units = [(h, n) for h in range(RET_HEADS) for n in range(RET_STEP_CHUNKS)]
    blk = lambda ref, h, n: ref[0, pl.ds(n * c, c), pl.ds(h * d, d)]
    qb, scores, kv = {}, {}, {}
    for h, n in units:
        qb[h, n] = blk(q_ref, h, n)
        k = blk(k_ref, h, n)
        scores[h, n] = lax.dot_general(qb[h, n], k, (((1,), (1,)), ((), ())),
                                       preferred_element_type=F32)
        kv[h, n] = lax.dot_general((k.astype(F32) * kdec_ref[h]).astype(BF16), blk(v_ref, h, n),
                                   (((0,), (0,)), ((), ())), preferred_element_type=F32)
    inner = {}
    for h, n in units:
        p = (scores[h, n] * dintra_ref[h]).astype(BF16)
        inner[h, n] = jnp.dot(p, blk(v_ref, h, n), preferred_element_type=F32)
    cross = {}
    for h in range(RET_HEADS):
        state = state_ref[h]
        for n in range(RET_STEP_CHUNKS):
            cross[h, n] = jnp.dot(qb[h, n], state.astype(BF16), preferred_element_type=F32)
            state = cdec_ref[h] * state + kv[h, n]
        state_ref[h] = state
    for h, n in units:
        o = inner[h, n] + cross[h, n] * qdec_ref[h]
        mu = jnp.mean(o, axis=-1, keepdims=True)
        oc = o - mu
        var = jnp.mean(oc * oc, axis=-1, keepdims=True)
        on = oc * lax.rsqrt(var + EPS) * gain_ref[:, pl.ds(h * d, d)]
        o_ref[0, pl.ds(n * c, c), pl.ds(h * d, d)] = (
            blk(g_ref, h, n).astype(F32) * on).astype(o_ref.dtype)


def _rotary_tables(s):
    pos = jnp.arange(s, dtype=F32)
    inv_freq = ROPE_BASE ** (-jnp.arange(0, RET_HEAD_DIM, 2, dtype=F32) / RET_HEAD_DIM)
    ang = pos[:, None] * inv_freq[None, :]
    cos, sin = jnp.cos(ang), jnp.sin(ang)
    return jnp.concatenate([cos, cos], axis=-1), jnp.concatenate([-sin, sin], axis=-1)


def _retention_tables():
    log_gamma = jnp.log1p(-jnp.exp2(-5.0 - jnp.arange(RET_HEADS, dtype=F32)))
    idx = jnp.arange(RET_CHUNK, dtype=F32)
    rel = idx[:, None] - idx[None, :]
    dintra = jnp.where(rel >= 0, jnp.exp(jnp.maximum(rel, 0.0)[None] * log_gamma[:, None, None]), 0.0)
    kdec = jnp.exp((RET_CHUNK - 1 - idx)[None] * log_gamma[:, None])
    qdec = jnp.exp((idx + 1)[None] * log_gamma[:, None])
    cdec = jnp.exp(RET_CHUNK * log_gamma)
    bc = lambda a: jnp.broadcast_to(a[:, :, None], (RET_HEADS, RET_CHUNK, RET_HEAD_DIM))
    cdec = jnp.broadcast_to(cdec[:, None, None], (RET_HEADS, 1, RET_HEAD_DIM))
    return dintra, bc(kdec), bc(qdec), cdec


def _retention(proj, gain, b, s):
    ts = RET_STEP_CHUNKS * RET_CHUNK
    c, d = RET_CHUNK, RET_HEAD_DIM
    base = (6 * MIX_W) // MIX_W
    dintra, kdec, qdec, cdec = _retention_tables()
    col = lambda off: pl.BlockSpec((1, ts, MIX_W), lambda bi, i: (bi, i, base + off))
    full = lambda shape: pl.BlockSpec(shape, lambda bi, i: (0,) * len(shape))
    return pl.pallas_call(
        _ret_kernel,
        grid=(b, s // ts),
        in_specs=[col(0), col(1), col(2), col(3), full((1, MIX_W)),
                  full((RET_HEADS, c, c)), full((RET_HEADS, c, d)), full((RET_HEADS, c, d)),
                  full((RET_HEADS, 1, d))],
        out_specs=pl.BlockSpec((1, ts, MIX_W), lambda bi, i: (bi, i, 0)),
        out_shape=jax.ShapeDtypeStruct((b, s, MIX_W), BF16),
        scratch_shapes=[pltpu.VMEM((RET_HEADS, d, d), F32)],
        compiler_params=_params(("parallel", "arbitrary")),
        name="retention",
    )(proj, proj, proj, proj, gain.reshape(1, MIX_W), dintra, kdec, qdec, cdec)


MERGE_TM = 512
HALO = 8


def _merge_kernel(x_ref, cb_ref, cc_ref, ch_ref, ccp_ref, chp_ref, gate0_ref, gate1_ref,
                  gate2_ref, sb_ref, ret_ref, convw_ref, convb_ref, wbr_ref, wo_ref, o_ref):
    tm = MERGE_TM
    gate_refs = (gate0_ref, gate1_ref, gate2_ref)
    u = cc_ref[0].astype(F32) * ch_ref[0].astype(F32)
    prev = ccp_ref[0].astype(F32) * chp_ref[0].astype(F32)
    prev = jnp.where(pl.program_id(1) == 0, 0.0, prev)
    row = lax.broadcasted_iota(jnp.int32, (tm, MIX_W), 0)
    u1 = jnp.where(row == 0, prev[HALO - 1:HALO, :], pltpu.roll(u, 1, 0))
    u2 = jnp.where(row == 0, prev[HALO - 2:HALO - 1, :],
                   jnp.where(row == 1, prev[HALO - 1:HALO, :], pltpu.roll(u, 2, 0)))
    conv = convb_ref[...] + u2 * convw_ref[0:1, :] + u1 * convw_ref[1:2, :] + u * convw_ref[2:3, :]
    conv_out = cb_ref[0].astype(F32) * conv
    branches = (conv_out.astype(BF16), sb_ref[0], ret_ref[0])
    merged = jnp.zeros((tm, D_MODEL), F32)
    for n in range(N_BRANCH):
        up = jnp.dot(branches[n], wbr_ref[n], preferred_element_type=F32)
        merged = merged + gate_refs[n][0].astype(F32) * up
    o_ref[0] = x_ref[0] + jnp.dot(merged.astype(BF16), wo_ref[...], preferred_element_type=F32)


def _merge(x, proj, sb_out, ret_out, conv_w, conv_b, w_branch, w_o):
    b, s, d = x.shape
    tm = MERGE_TM
    w = MIX_W
    gate_blk = (10 * MIX_W) // d
    tok = lambda width, cblk: pl.BlockSpec((1, tm, width), lambda bi, i: (bi, i, cblk))
    halo = lambda cblk: pl.BlockSpec(
        (1, HALO, w), lambda bi, i: (bi, jnp.maximum(i * (tm // HALO) - 1, 0), cblk))
    full = lambda shape: pl.BlockSpec(shape, lambda bi, i: (0,) * len(shape))
    return pl.pallas_call(
        _merge_kernel,
        grid=(b, s // tm),
        in_specs=[tok(d, 0), tok(w, 0), tok(w, 1), tok(w, 2), halo(1), halo(2),
                  tok(d, gate_blk), tok(d, gate_blk + 1), tok(d, gate_blk + 2), tok(w, 0), tok(w, 0),
                  full((CONV_K, w)), full((1, w)), full((N_BRANCH, w, d)), full((d, d))],
        out_specs=tok(d, 0),
        out_shape=jax.ShapeDtypeStruct((b, s, d), F32),
        compiler_params=_params(("parallel", "parallel")),
        name="merge",
    )(x, proj, proj, proj, proj, proj, proj, proj, proj, sb_out, ret_out, conv_w,
      conv_b.reshape(1, w), w_branch, w_o)


XA_TM = 1024


def _xattn_kernel(x_ref, g_ref, wq_ref, kv_ref, wo_ref, o_ref):
    x = x_ref[0]
    h = _rmsnorm_f32(x, g_ref[...]).astype(BF16)
    q = jnp.dot(h, wq_ref[...], preferred_element_type=F32)
    q = (q * (XA_HEAD_DIM ** -0.5)).astype(BF16)
    heads = [slice(hd * XA_HEAD_DIM, (hd + 1) * XA_HEAD_DIM) for hd in range(XA_HEADS)]
    scores = [lax.dot_general(q[:, cols], kv_ref[0, :, cols], (((1,), (1,)), ((), ())),
                              preferred_element_type=F32) for cols in heads]
    outs = []
    for cols, sc in zip(heads, scores):
        e = jnp.exp(sc - jnp.max(sc, axis=-1, keepdims=True))
        p = e / jnp.sum(e, axis=-1, keepdims=True)
        vh = kv_ref[0, :, D_MODEL + cols.start:D_MODEL + cols.stop]
        outs.append(jnp.dot(p.astype(BF16), vh, preferred_element_type=F32).astype(BF16))
    o = jnp.concatenate(outs, axis=-1)
    o_ref[0] = x + jnp.dot(o, wo_ref[...], preferred_element_type=F32)


def _xattn(x, g, w_q, kv, w_o):
    b, s, d = x.shape
    tm = XA_TM
    full = lambda shape: pl.BlockSpec(shape, lambda bi, i: (0,) * len(shape))
    return pl.pallas_call(
        _xattn_kernel,
        grid=(b, s // tm),
        in_specs=[pl.BlockSpec((1, tm, d), lambda bi, i: (bi, i, 0)), full((1, d)), full((d, d)),
                  pl.BlockSpec((1, MEM_LEN, 2 * d), lambda bi, i: (bi, 0, 0)), full((d, d))],
        out_specs=pl.BlockSpec((1, tm, d), lambda bi, i: (bi, i, 0)),
        out_shape=jax.ShapeDtypeStruct((b, s, d), F32),
        compiler_params=_params(("parallel", "parallel")),
        name="xattn",
    )(x, g.reshape(1, d), w_q, kv, w_o)


MLP_TM = 1024
MLP_TF = 1024


def _mlp_kernel(x_ref, g_ref, wu_ref, wd_ref, fg_ref, o_ref, h_ref, acc_ref, *, final_norm):
    j = pl.program_id(1)

    rows = [pl.ds(r * (MLP_TM // ROW_SPLIT), MLP_TM // ROW_SPLIT) for r in range(ROW_SPLIT)]

    def step(first):
        if first:
            for r in rows:
                h_ref[r, :] = _rmsnorm_f32(x_ref[r, :], g_ref[...]).astype(BF16)
        ups = [jnp.dot(h_ref[r, :], wu_ref[...], preferred_element_type=F32) for r in rows]
        acts = [jnp.square(jnp.maximum(up, 0.0)).astype(BF16) for up in ups]
        for r, act in zip(rows, acts):
            down = jnp.dot(act, wd_ref[...], preferred_element_type=F32)
            acc_ref[r, :] = (x_ref[r, :] if first else acc_ref[r, :]) + down

    pl.when(j == 0)(functools.partial(step, True))
    pl.when(j > 0)(functools.partial(step, False))

    @pl.when(j == pl.num_programs(1) - 1)
    def _():
        y = acc_ref[...]
        o_ref[...] = _rmsnorm_f32(y, fg_ref[...]) if final_norm else y


def _mlp(x2d, g, w_up, w_down, final_g, final_norm):
    m, d = x2d.shape
    f = w_up.shape[1]
    tm, tf = MLP_TM, MLP_TF
    return pl.pallas_call(
        functools.partial(_mlp_kernel, final_norm=final_norm),
        grid=(m // tm, f // tf),
        in_specs=[pl.BlockSpec((tm, d), lambda i, j: (i, 0)),
                  pl.BlockSpec((1, d), lambda i, j: (0, 0)),
                  pl.BlockSpec((d, tf), lambda i, j: (0, j)),
                  pl.BlockSpec((tf, d), lambda i, j: (j, 0)),
                  pl.BlockSpec((1, d), lambda i, j: (0, 0))],
        out_specs=pl.BlockSpec((tm, d), lambda i, j: (i, 0)),
        out_shape=jax.ShapeDtypeStruct((m, d), F32),
        scratch_shapes=[pltpu.VMEM((tm, d), BF16), pltpu.VMEM((tm, d), F32)],
        compiler_params=_params(("parallel", "arbitrary")),
        name="mlp",
    )(x2d, g.reshape(1, d), w_up, w_down, final_g.reshape(1, d))


def kernel(x, mem, norm_mix_g, w_in, b_gate, conv_w, conv_b, ret_norm_g, w_branch, w_o,
           norm_xa_g, norm_mem_g, w_xq, w_xkv, w_xo, norm_mlp_g, w_up, w_down, final_g):
    b, s, d = x.shape
    depth = w_in.shape[0]
    assert d == D_MODEL and b % SB_BATCH == 0 and (b * s) % MLP_TM == 0
    assert all(s % tile == 0 for tile in (SB_T, MERGE_TM, XA_TM, IN_TM, RET_STEP_CHUNKS * RET_CHUNK))
    mem2d = mem.reshape(b * MEM_LEN, d)
    col_scale = jnp.ones((IN_COLS,), F32).at[3 * MIX_W:4 * MIX_W].set(SB_HEAD_DIM ** -0.5 * LOG2E)
    cos2, sin2 = _rotary_tables(s)
    for l in range(depth):
        w_in_l = (w_in[l] * col_scale).astype(BF16)
        proj = _in_proj(x.reshape(b * s, d), norm_mix_g[l], w_in_l, b_gate[l], cos2, sin2, s)
        proj = proj.reshape(b, s, IN_COLS)
        sb_out = _sb_attention(proj, b, s)
        ret_out = _retention(proj, ret_norm_g[l], b, s)
        x = _merge(x, proj, sb_out, ret_out, conv_w[l], conv_b[l], w_branch[l].astype(BF16),
                   w_o[l].astype(BF16))
        kv = _norm_matmul(mem2d, norm_mem_g[l], w_xkv[l].astype(BF16), b * MEM_LEN, 1024)
        x = _xattn(x, norm_xa_g[l], w_xq[l].astype(BF16), kv.reshape(b, MEM_LEN, 2 * d),
                   w_xo[l].astype(BF16))
        x = _mlp(x.reshape(b * s, d), norm_mlp_g[l], w_up[l].astype(BF16), w_down[l].astype(BF16),
                 final_g, l == depth - 1).reshape(b, s, d)
    return x
```

```python
import functools
import math

import jax
import jax.numpy as jnp
from jax import lax
from jax.experimental import pallas as pl
from jax.experimental.pallas import tpu as pltpu

D_MODEL = 1024
MEM_LEN = 256
N_BRANCH = 3
MIX_W = D_MODEL // 2
CONV_K = 3
SB_HEADS = 8
SB_HEAD_DIM = MIX_W // SB_HEADS
RET_HEADS = 4
RET_HEAD_DIM = MIX_W // RET_HEADS
RET_CHUNK = 128
XA_HEADS = 4
XA_HEAD_DIM = D_MODEL // XA_HEADS
D_FF = 4 * D_MODEL
ROPE_BASE = 10000.0
EPS = 1e-6
IN_COLS = 10 * MIX_W + N_BRANCH * D_MODEL

LANES = 128
VMEM_LIMIT = 48 * 1024 * 1024

F32 = jnp.float32
BF16 = jnp.bfloat16


def _params(semantics):
    return pltpu.CompilerParams(dimension_semantics=semantics, vmem_limit_bytes=VMEM_LIMIT)


def _rmsnorm_f32(xf, g):
    return xf * lax.rsqrt(jnp.mean(xf * xf, axis=-1, keepdims=True) + EPS) * g


def _norm_matmul_kernel(x_ref, g_ref, w_ref, o_ref, h_ref):
    @pl.when(pl.program_id(1) == 0)
    def _():
        h_ref[...] = _rmsnorm_f32(x_ref[...], g_ref[...]).astype(BF16)

    o_ref[...] = jnp.dot(h_ref[...], w_ref[...], preferred_element_type=F32).astype(o_ref.dtype)


def _norm_matmul(x2d, g, w, tm, tn):
    m, d = x2d.shape
    n = w.shape[1]
    return pl.pallas_call(
        _norm_matmul_kernel,
        grid=(m // tm, n // tn),
        in_specs=[
            pl.BlockSpec((tm, d), lambda i, j: (i, 0)),
            pl.BlockSpec((1, d), lambda i, j: (0, 0)),
            pl.BlockSpec((d, tn), lambda i, j: (0, j)),
        ],
        out_specs=pl.BlockSpec((tm, tn), lambda i, j: (i, j)),
        out_shape=jax.ShapeDtypeStruct((m, n), BF16),
        scratch_shapes=[pltpu.VMEM((tm, d), BF16)],
        compiler_params=_params(("parallel", "arbitrary")),
        name="norm_matmul",
    )(x2d, g.reshape(1, d), w)


IN_TM = 1024
IN_GROUPS = 4
IN_TN = 4 * MIX_W
assert IN_COLS == 4 * IN_TN


def _rotary(tf, cos2, sin2):
    return tf * cos2 + pltpu.roll(tf, RET_HEAD_DIM // 2, 1) * sin2


def _in_proj_kernel(x_ref, g_ref, w_ref, bias_ref, cos_ref, sin_ref, o_ref, h_ref):
    j = pl.program_id(1)
    w = MIX_W
    groups = [pl.ds(r * (IN_TM // IN_GROUPS), IN_TM // IN_GROUPS) for r in range(IN_GROUPS)]

    def project():
        return [(rows, jnp.dot(h_ref[rows, :], w_ref[...], preferred_element_type=F32))
                for rows in groups]

    @pl.when(j == 0)
    def _():
        for rows in groups:
            h_ref[rows, :] = _rmsnorm_f32(x_ref[rows, :], g_ref[...]).astype(BF16)
        for rows, acc in project():
            o_ref[rows, :] = acc.astype(BF16)

    @pl.when(j == 1)
    def _():
        for rows, acc in project():
            o_ref[rows, :2 * w] = acc[:, :2 * w].astype(BF16)
            cos2, sin2 = cos_ref[rows, :], sin_ref[rows, :]
            for start, scale in ((2 * w, None), (3 * w, RET_HEAD_DIM ** -0.5)):
                for h in range(RET_HEADS):
                    cols = slice(start + h * RET_HEAD_DIM, start + (h + 1) * RET_HEAD_DIM)
                    r = _rotary(acc[:, cols], cos2, sin2)
                    o_ref[rows, cols] = (r if scale is None else r * scale).astype(BF16)

    @pl.when(j == 2)
    def _():
        for rows, acc in project():
            o_ref[rows, :w] = acc[:, :w].astype(BF16)
            gate = acc[:, w:2 * w]
            o_ref[rows, w:2 * w] = (gate * jax.nn.sigmoid(gate)).astype(BF16)
            o_ref[rows, 2 * w:] = jax.nn.sigmoid(acc[:, 2 * w:] + bias_ref[:, 2 * w:]).astype(BF16)

    @pl.when(j == 3)
    def _():
        for rows, acc in project():
            o_ref[rows, :] = jax.nn.sigmoid(acc + bias_ref[...]).astype(BF16)


def _in_proj(x2d, g, w, b_gate, cos2, sin2, s):
    m, d = x2d.shape
    tm, tn = IN_TM, IN_TN
    bias = jnp.concatenate([jnp.zeros((IN_COLS - N_BRANCH * D_MODEL,), F32), b_gate.reshape(-1)])
    seq_blocks = s // tm
    return pl.pallas_call(
        _in_proj_kernel,
        grid=(m // tm, IN_COLS // tn),
        in_specs=[
            pl.BlockSpec((tm, d), lambda i, j: (i, 0)),
            pl.BlockSpec((1, d), lambda i, j: (0, 0)),
            pl.BlockSpec((d, tn), lambda i, j: (0, j)),
            pl.BlockSpec((1, tn), lambda i, j: (0, j)),
            pl.BlockSpec((tm, RET_HEAD_DIM), lambda i, j: (i % seq_blocks, 0)),
            pl.BlockSpec((tm, RET_HEAD_DIM), lambda i, j: (i % seq_blocks, 0)),
        ],
        out_specs=pl.BlockSpec((tm, tn), lambda i, j: (i, j)),
        out_shape=jax.ShapeDtypeStruct((m, IN_COLS), BF16),
        scratch_shapes=[pltpu.VMEM((tm, d), BF16)],
        compiler_params=_params(("parallel", "arbitrary")),
        name="in_proj",
    )(x2d, g.reshape(1, d), w, bias.reshape(1, IN_COLS), cos2, sin2)


SB_T = 256
SB_BATCH = 4
LOG2E = math.log2(math.e)
LOG2E_HI = 1.4375
LOG2E_LO = LOG2E - LOG2E_HI


def _sb_scores(qs, k_ref, ks, z_ref):
    for i, q in enumerate(qs):
        kb = k_ref[i // 2, pl.ds(ks, SB_T), :]
        z_ref[i] = lax.dot_general(q, kb, (((1,), (1,)), ((), ())), preferred_element_type=F32)


def _sb_values(v_ref, ks, a_ref, p_ref, acc_ref):
    for i in range(a_ref.shape[0]):
        vb = v_ref[i // 2, pl.ds(ks, SB_T), :]
        acc_ref[i] += p_ref[i] * jnp.dot(a_ref[i], vb, preferred_element_type=F32)


def _sb_stage1(z2, upper):
    zb = z2.astype(BF16)
    hi = jnp.maximum(zb, 0.0)
    lo = zb - hi
    ln1p = jnp.log(1.0 + jnp.exp2(lo - hi))
    log1p = ln1p * LOG2E_HI + ln1p * LOG2E_LO
    sp = hi + log1p
    suffix = jnp.dot(sp, upper, preferred_element_type=F32)
    return (lo - log1p).astype(F32), suffix, suffix[:, 0:1] + sp[:, 0:1].astype(F32)


def _sb_step(qs, k_ref, v_ref, ks, upper, z_ref, a_ref, p_ref, c_ref, acc_ref, first):
    t = SB_T
    ks_next = pl.multiple_of(jnp.maximum(ks - t, 0), t)
    stage1 = []
    for i, q in enumerate(qs):
        z2 = z_ref[i]
        _sb_scores(qs[i:i + 1], k_ref.at[i // 2:i // 2 + 1], ks_next, z_ref.at[i:i + 1])
        if not first:
            _sb_values(v_ref.at[i // 2:i // 2 + 1], pl.multiple_of(ks + t, t), a_ref.at[i:i + 1],
                       p_ref.at[i:i + 1], acc_ref.at[i:i + 1])
        stage1.append(_sb_stage1(z2, upper))
    for i, (log_beta, suffix, total) in enumerate(stage1):
        neg_carry = c_ref[i]
        a_ref[i] = jnp.exp2(log_beta - suffix).astype(BF16)
        p_ref[i] = jnp.exp2(neg_carry)
        c_ref[i] = neg_carry - total


def _sb_kernel(q_ref, k_ref, v_ref, o_ref, z_ref, a_ref, p_ref, c_ref, acc_ref):
    t = SB_T
    nb = q_ref.shape[0]
    qi = pl.program_id(1)
    lane = lax.broadcasted_iota(jnp.int32, (t, LANES), 1)
    first = lane < SB_HEAD_DIM
    qs = []
    for bi in range(nb):
        q2 = q_ref[bi]
        zero = jnp.zeros_like(q2)
        qs += [jnp.where(first, q2, zero), jnp.where(first, zero, q2)]
    row = lax.broadcasted_iota(jnp.int32, (t, t), 0)
    col = lax.broadcasted_iota(jnp.int32, (t, t), 1)
    upper = jnp.where(row > col, 1.0, 0.0).astype(BF16)
    causal = col < row

    c_ref[...] = jnp.zeros_like(c_ref)
    acc_ref[...] = jnp.zeros_like(acc_ref)
    diag = pl.multiple_of(qi * t, t)
    for i, q in enumerate(qs):
        z = lax.dot_general(q, k_ref[i // 2, pl.ds(diag, t), :], (((1,), (1,)), ((), ())),
                            preferred_element_type=F32)
        z_ref[i] = jnp.where(causal, z, -jnp.inf)
    _sb_step(qs, k_ref, v_ref, diag, upper, z_ref, a_ref, p_ref, c_ref, acc_ref, True)

    @pl.loop(0, qi)
    def _(j):
        ks = pl.multiple_of((qi - 1 - j) * t, t)
        _sb_step(qs, k_ref, v_ref, ks, upper, z_ref, a_ref, p_ref, c_ref, acc_ref, False)

    _sb_values(v_ref, 0, a_ref, p_ref, acc_ref)
    for bi in range(nb):
        o_ref[bi] = jnp.where(first, acc_ref[2 * bi], acc_ref[2 * bi + 1]).astype(o_ref.dtype)


def _sb_attention(proj, b, s):
    t, nb = SB_T, SB_BATCH
    ns = 2 * nb
    qc, kc, vc = (3 * MIX_W) // LANES, (4 * MIX_W) // LANES, (5 * MIX_W) // LANES
    pairs = MIX_W // LANES
    return pl.pallas_call(
        _sb_kernel,
        grid=(b // nb * pairs, s // t),
        in_specs=[
            pl.BlockSpec((nb, t, LANES), lambda g, i: (g // pairs, i, qc + g % pairs)),
            pl.BlockSpec((nb, s, LANES), lambda g, i: (g // pairs, 0, kc + g % pairs)),
            pl.BlockSpec((nb, s, LANES), lambda g, i: (g // pairs, 0, vc + g % pairs)),
        ],
        out_specs=pl.BlockSpec((nb, t, LANES), lambda g, i: (g // pairs, i, g % pairs)),
        out_shape=jax.ShapeDtypeStruct((b, s, MIX_W), BF16),
        scratch_shapes=[pltpu.VMEM((ns, t, t), F32), pltpu.VMEM((ns, t, t), BF16),
                        pltpu.VMEM((ns, t, 1), F32), pltpu.VMEM((ns, t, 1), F32),
                        pltpu.VMEM((ns, t, LANES), F32)],
        compiler_params=_params(("parallel", "arbitrary")),
        name="sb_attention",
    )(proj, proj, proj)


RET_STEP_CHUNKS = 8


def _ret_kernel(q_ref, k_ref, v_ref, g_ref, gain_ref, dintra_ref, kdec_ref, qdec_ref, cdec_ref,
                o_ref, state_ref):
    c, d = RET_CHUNK, RET_HEAD_DIM

    @pl.when(pl.program_id(1) == 0)
    def _():
        state_ref[...] = jnp.zeros_like(state_ref)

    units = [(h, n) for h in range(RET_HEADS) for n in range(RET_STEP_CHUNKS)]
    blk = lambda ref, h, n: ref[0, pl.ds(n * c, c), pl.ds(h * d, d)]
    qb, scores, kv = {}, {}, {}
    for h, n in units:
        qb[h, n] = blk(q_ref, h, n)
        k = blk(k_ref, h, n)
        scores[h, n] = lax.dot_general(qb[h, n], k, (((1,), (1,)), ((), ())),
                                       preferred_element_type=F32)
        kv[h, n] = lax.dot_general((k.astype(F32) * kdec_ref[h]).astype(BF16), blk(v_ref, h, n),
                                   (((0,), (0,)), ((), ())), preferred_element_type=F32)
    inner = {}
    for h, n in units:
        p = (scores[h, n] * dintra_ref[h]).astype(BF16)
        inner[h, n] = jnp.dot(p, blk(v_ref, h, n), preferred_element_type=F32)
    cross = {}
    for h in range(RET_HEADS):
        state = state_ref[h]
        for n in range(RET_STEP_CHUNKS):
            cross[h, n] = jnp.dot(qb[h, n], state.astype(BF16), preferred_element_type=F32)
            state = cdec_ref[h] * state + kv[h, n]
        state_ref[h] = state
    for h, n in units:
        o = inner[h, n] + cross[h, n] * qdec_ref[h]
        mu = jnp.mean(o, axis=-1, keepdims=True)
        oc = o - mu
        var = jnp.mean(oc * oc, axis=-1, keepdims=True)
        on = oc * lax.rsqrt(var + EPS) * gain_ref[:, pl.ds(h * d, d)]
        o_ref[0, pl.ds(n * c, c), pl.ds(h * d, d)] = (
            blk(g_ref, h, n).astype(F32) * on).astype(o_ref.dtype)


def _rotary_tables(s):
    pos = jnp.arange(s, dtype=F32)
    inv_freq = ROPE_BASE ** (-jnp.arange(0, RET_HEAD_DIM, 2, dtype=F32) / RET_HEAD_DIM)
    ang = pos[:, None] * inv_freq[None, :]
    cos, sin = jnp.cos(ang), jnp.sin(ang)
    return jnp.concatenate([cos, cos], axis=-1), jnp.concatenate([-sin, sin], axis=-1)


def _retention_tables():
    log_gamma = jnp.log1p(-jnp.exp2(-5.0 - jnp.arange(RET_HEADS, dtype=F32)))
    idx = jnp.arange(RET_CHUNK, dtype=F32)
    rel = idx[:, None] - idx[None, :]
    dintra = jnp.where(rel >= 0, jnp.exp(jnp.maximum(rel, 0.0)[None] * log_gamma[:, None, None]), 0.0)
    kdec = jnp.exp((RET_CHUNK - 1 - idx)[None] * log_gamma[:, None])
    qdec = jnp.exp((idx + 1)[None] * log_gamma[:, None])
    cdec = jnp.exp(RET_CHUNK * log_gamma)
    bc = lambda a: jnp.broadcast_to(a[:, :, None], (RET_HEADS, RET_CHUNK, RET_HEAD_DIM))
    cdec = jnp.broadcast_to(cdec[:, None, None], (RET_HEADS, 1, RET_HEAD_DIM))
    return dintra, bc(kdec), bc(qdec), cdec


def _retention(proj, gain, b, s):
    ts = RET_STEP_CHUNKS * RET_CHUNK
    c, d = RET_CHUNK, RET_HEAD_DIM
    base = (6 * MIX_W) // MIX_W
    dintra, kdec, qdec, cdec = _retention_tables()
    col = lambda off: pl.BlockSpec((1, ts, MIX_W), lambda bi, i: (bi, i, base + off))
    full = lambda shape: pl.BlockSpec(shape, lambda bi, i: (0,) * len(shape))
    return pl.pallas_call(
        _ret_kernel,
        grid=(b, s // ts),
        in_specs=[col(0), col(1), col(2), col(3), full((1, MIX_W)),
                  full((RET_HEADS, c, c)), full((RET_HEADS, c, d)), full((RET_HEADS, c, d)),
                  full((RET_HEADS, 1, d))],
        out_specs=pl.BlockSpec((1, ts, MIX_W), lambda bi, i: (bi, i, 0)),
        out_shape=jax.ShapeDtypeStruct((b, s, MIX_W), BF16),
        scratch_shapes=[pltpu.VMEM((RET_HEADS, d, d), F32)],
        compiler_params=_params(("parallel", "arbitrary")),
        name="retention",
    )(proj, proj, proj, proj, gain.reshape(1, MIX_W), dintra, kdec, qdec, cdec)


MERGE_TM = 512
HALO = 8


def _merge_kernel(x_ref, cb_ref, cc_ref, ch_ref, ccp_ref, chp_ref, gate0_ref, gate1_ref,
                  gate2_ref, sb_ref, ret_ref, convw_ref, convb_ref, wbr_ref, wo_ref, o_ref):
    tm = MERGE_TM
    gate_refs = (gate0_ref, gate1_ref, gate2_ref)
    u = cc_ref[0].astype(F32) * ch_ref[0].astype(F32)
    prev = ccp_ref[0].astype(F32) * chp_ref[0].astype(F32)
    prev = jnp.where(pl.program_id(1) == 0, 0.0, prev)
    row = lax.broadcasted_iota(jnp.int32, (tm, MIX_W), 0)
    u1 = jnp.where(row == 0, prev[HALO - 1:HALO, :], pltpu.roll(u, 1, 0))
    u2 = jnp.where(row == 0, prev[HALO - 2:HALO - 1, :],
                   jnp.where(row == 1, prev[HALO - 1:HALO, :], pltpu.roll(u, 2, 0)))
    conv = convb_ref[...] + u2 * convw_ref[0:1, :] + u1 * convw_ref[1:2, :] + u * convw_ref[2:3, :]
    conv_out = cb_ref[0].astype(F32) * conv
    branches = (conv_out.astype(BF16), sb_ref[0], ret_ref[0])
    merged = jnp.zeros((tm, D_MODEL), F32)
    for n in range(N_BRANCH):
        up = jnp.dot(branches[n], wbr_ref[n], preferred_element_type=F32)
        merged = merged + gate_refs[n][0].astype(F32) * up
    o_ref[0] = x_ref[0] + jnp.dot(merged.astype(BF16), wo_ref[...], preferred_element_type=F32)


def _merge(x, proj, sb_out, ret_out, conv_w, conv_b, w_branch, w_o):
    b, s, d = x.shape
    tm = MERGE_TM
    w = MIX_W
    gate_blk = (10 * MIX_W) // d
    tok = lambda width, cblk: pl.BlockSpec((1, tm, width), lambda bi, i: (bi, i, cblk))
    halo = lambda cblk: pl.BlockSpec(
        (1, HALO, w), lambda bi, i: (bi, jnp.maximum(i * (tm // HALO) - 1, 0), cblk))
    full = lambda shape: pl.BlockSpec(shape, lambda bi, i: (0,) * len(shape))
    return pl.pallas_call(
        _merge_kernel,
        grid=(b, s // tm),
        in_specs=[tok(d, 0), tok(w, 0), tok(w, 1), tok(w, 2), halo(1), halo(2),
                  tok(d, gate_blk), tok(d, gate_blk + 1), tok(d, gate_blk + 2), tok(w, 0), tok(w, 0),
                  full((CONV_K, w)), full((1, w)), full((N_BRANCH, w, d)), full((d, d))],
        out_specs=tok(d, 0),
        out_shape=jax.ShapeDtypeStruct((b, s, d), F32),
        compiler_params=_params(("parallel", "parallel")),
        name="merge",
    )(x, proj, proj, proj, proj, proj, proj, proj, proj, sb_out, ret_out, conv_w,
      conv_b.reshape(1, w), w_branch, w_o)


XA_TM = 1024


def _xattn_kernel(x_ref, g_ref, wq_ref, kv_ref, wo_ref, o_ref):
    x = x_ref[0]
    h = _rmsnorm_f32(x, g_ref[...]).astype(BF16)
    q = jnp.dot(h, wq_ref[...], preferred_element_type=F32)
    q = (q * (XA_HEAD_DIM ** -0.5)).astype(BF16)
    heads = [slice(hd * XA_HEAD_DIM, (hd + 1) * XA_HEAD_DIM) for hd in range(XA_HEADS)]
    scores = [lax.dot_general(q[:, cols], kv_ref[0, :, cols], (((1,), (1,)), ((), ())),
                              preferred_element_type=F32) for cols in heads]
    outs = []
    for cols, sc in zip(heads, scores):
        e = jnp.exp(sc - jnp.max(sc, axis=-1, keepdims=True))
        p = e / jnp.sum(e, axis=-1, keepdims=True)
        vh = kv_ref[0, :, D_MODEL + cols.start:D_MODEL + cols.stop]
        outs.append(jnp.dot(p.astype(BF16), vh, preferred_element_type=F32).astype(BF16))
    o = jnp.concatenate(outs, axis=-1)
    o_ref[0] = x + jnp.dot(o, wo_ref[...], preferred_element_type=F32)


def _xattn(x, g, w_q, kv, w_o):
    b, s, d = x.shape
    tm = XA_TM
    full = lambda shape: pl.BlockSpec(shape, lambda bi, i: (0,) * len(shape))
    return pl.pallas_call(
        _xattn_kernel,
        grid=(b, s // tm),
        in_specs=[pl.BlockSpec((1, tm, d), lambda bi, i: (bi, i, 0)), full((1, d)), full((d, d)),
                  pl.BlockSpec((1, MEM_LEN, 2 * d), lambda bi, i: (bi, 0, 0)), full((d, d))],
        out_specs=pl.BlockSpec((1, tm, d), lambda bi, i: (bi, i, 0)),
        out_shape=jax.ShapeDtypeStruct((b, s, d), F32),
        compiler_params=_params(("parallel", "parallel")),
        name="xattn",
    )(x, g.reshape(1, d), w_q, kv, w_o)


MLP_TM = 1024
MLP_TF = 1024
MLP_GROUPS = 2


def _mlp_kernel(x_ref, g_ref, wu_ref, wd_ref, fg_ref, o_ref, h_ref, acc_ref, *, final_norm):
    j = pl.program_id(1)

    rows = [pl.ds(r * (MLP_TM // MLP_GROUPS), MLP_TM // MLP_GROUPS) for r in range(MLP_GROUPS)]

    def step(first):
        if first:
            for r in rows:
                h_ref[r, :] = _rmsnorm_f32(x_ref[r, :], g_ref[...]).astype(BF16)
        ups = [jnp.dot(h_ref[r, :], wu_ref[...], preferred_element_type=F32) for r in rows]
        acts = [jnp.square(jnp.maximum(up, 0.0)).astype(BF16) for up in ups]
        for r, act in zip(rows, acts):
            down = jnp.dot(act, wd_ref[...], preferred_element_type=F32)
            acc_ref[r, :] = (x_ref[r, :] if first else acc_ref[r, :]) + down

    pl.when(j == 0)(functools.partial(step, True))
    pl.when(j > 0)(functools.partial(step, False))

    @pl.when(j == pl.num_programs(1) - 1)
    def _():
        y = acc_ref[...]
        o_ref[...] = _rmsnorm_f32(y, fg_ref[...]) if final_norm else y


def _mlp(x2d, g, w_up, w_down, final_g, final_norm):
    m, d = x2d.shape
    f = w_up.shape[1]
    tm, tf = MLP_TM, MLP_TF
    return pl.pallas_call(
        functools.partial(_mlp_kernel, final_norm=final_norm),
        grid=(m // tm, f // tf),
        in_specs=[pl.BlockSpec((tm, d), lambda i, j: (i, 0)),
                  pl.BlockSpec((1, d), lambda i, j: (0, 0)),
                  pl.BlockSpec((d, tf), lambda i, j: (0, j)),
                  pl.BlockSpec((tf, d), lambda i, j: (j, 0)),
                  pl.BlockSpec((1, d), lambda i, j: (0, 0))],
        out_specs=pl.BlockSpec((tm, d), lambda i, j: (i, 0)),
        out_shape=jax.ShapeDtypeStruct((m, d), F32),
        scratch_shapes=[pltpu.VMEM((tm, d), BF16), pltpu.VMEM((tm, d), F32)],
        compiler_params=_params(("parallel", "arbitrary")),
        name="mlp",
    )(x2d, g.reshape(1, d), w_up, w_down, final_g.reshape(1, d))


def kernel(x, mem, norm_mix_g, w_in, b_gate, conv_w, conv_b, ret_norm_g, w_branch, w_o,
           norm_xa_g, norm_mem_g, w_xq, w_xkv, w_xo, norm_mlp_g, w_up, w_down, final_g):
    b, s, d = x.shape
    depth = w_in.shape[0]
    assert d == D_MODEL and b % SB_BATCH == 0 and (b * s) % MLP_TM == 0
    assert all(s % tile == 0 for tile in (SB_T, MERGE_TM, XA_TM, IN_TM, RET_STEP_CHUNKS * RET_CHUNK))
    mem2d = mem.reshape(b * MEM_LEN, d)
    col_scale = jnp.ones((IN_COLS,), F32).at[3 * MIX_W:4 * MIX_W].set(SB_HEAD_DIM ** -0.5 * LOG2E)
    cos2, sin2 = _rotary_tables(s)
    for l in range(depth):
        w_in_l = (w_in[l] * col_scale).astype(BF16)
        proj = _in_proj(x.reshape(b * s, d), norm_mix_g[l], w_in_l, b_gate[l], cos2, sin2, s)
        proj = proj.reshape(b, s, IN_COLS)
        sb_out = _sb_attention(proj, b, s)
        ret_out = _retention(proj, ret_norm_g[l], b, s)
        x = _merge(x, proj, sb_out, ret_out, conv_w[l], conv_b[l], w_branch[l].astype(BF16),
                   w_o[l].astype(BF16))
        kv = _norm_matmul(mem2d, norm_mem_g[l], w_xkv[l].astype(BF16), b * MEM_LEN, 1024)
        x = _xattn(x, norm_xa_g[l], w_xq[l].astype(BF16), kv.reshape(b, MEM_LEN, 2 * d),
                   w_xo[l].astype(BF16))
        x = _mlp(x.reshape(b * s, d), norm_mlp_g[l], w_up[l].astype(BF16), w_down[l].astype(BF16),
                 final_g, l == depth - 1).reshape(b, s, d)
    return x
```

```python
import functools
import math

import jax
import jax.numpy as jnp
from jax import lax
from jax.experimental import pallas as pl
from jax.experimental.pallas import tpu as pltpu

D_MODEL = 1024
MEM_LEN = 256
N_BRANCH = 3
MIX_W = D_MODEL // 2
CONV_K = 3
SB_HEADS = 8
SB_HEAD_DIM = MIX_W // SB_HEADS
RET_HEADS = 4
RET_HEAD_DIM = MIX_W // RET_HEADS
RET_CHUNK = 128
XA_HEADS = 4
XA_HEAD_DIM = D_MODEL // XA_HEADS
D_FF = 4 * D_MODEL
ROPE_BASE = 10000.0
EPS = 1e-6
IN_COLS = 10 * MIX_W + N_BRANCH * D_MODEL

LANES = 128
VMEM_LIMIT = 48 * 1024 * 1024

F32 = jnp.float32
BF16 = jnp.bfloat16


def _params(semantics):
    return pltpu.CompilerParams(dimension_semantics=semantics, vmem_limit_bytes=VMEM_LIMIT)


def _rmsnorm_f32(xf, g):
    return xf * lax.rsqrt(jnp.mean(xf * xf, axis=-1, keepdims=True) + EPS) * g


def _norm_matmul_kernel(x_ref, g_ref, w_ref, o_ref, h_ref):
    @pl.when(pl.program_id(1) == 0)
    def _():
        h_ref[...] = _rmsnorm_f32(x_ref[...], g_ref[...]).astype(BF16)

    o_ref[...] = jnp.dot(h_ref[...], w_ref[...], preferred_element_type=F32).astype(o_ref.dtype)


def _norm_matmul(x2d, g, w, tm, tn):
    m, d = x2d.shape
    n = w.shape[1]
    return pl.pallas_call(
        _norm_matmul_kernel,
        grid=(m // tm, n // tn),
        in_specs=[
            pl.BlockSpec((tm, d), lambda i, j: (i, 0)),
            pl.BlockSpec((1, d), lambda i, j: (0, 0)),
            pl.BlockSpec((d, tn), lambda i, j: (0, j)),
        ],
        out_specs=pl.BlockSpec((tm, tn), lambda i, j: (i, j)),
        out_shape=jax.ShapeDtypeStruct((m, n), BF16),
        scratch_shapes=[pltpu.VMEM((tm, d), BF16)],
        compiler_params=_params(("parallel", "arbitrary")),
        name="norm_matmul",
    )(x2d, g.reshape(1, d), w)


IN_TM = 1024
IN_GROUPS = 4
IN_TN = 4 * MIX_W
assert IN_COLS == 4 * IN_TN


def _rotary(tf, cos2, sin2):
    return tf * cos2 + pltpu.roll(tf, RET_HEAD_DIM // 2, 1) * sin2


def _in_proj_kernel(x_ref, g_ref, w_ref, bias_ref, cos_ref, sin_ref, o_ref, h_ref):
    j = pl.program_id(1)
    w = MIX_W
    groups = [pl.ds(r * (IN_TM // IN_GROUPS), IN_TM // IN_GROUPS) for r in range(IN_GROUPS)]

    def project():
        return [(rows, jnp.dot(h_ref[rows, :], w_ref[...], preferred_element_type=F32))
                for rows in groups]

    @pl.when(j == 0)
    def _():
        for rows in groups:
            h_ref[rows, :] = _rmsnorm_f32(x_ref[rows, :], g_ref[...]).astype(BF16)
        for rows, acc in project():
            o_ref[rows, :] = acc.astype(BF16)

    @pl.when(j == 1)
    def _():
        for rows, acc in project():
            o_ref[rows, :2 * w] = acc[:, :2 * w].astype(BF16)
            cos2, sin2 = cos_ref[rows, :], sin_ref[rows, :]
            for start, scale in ((2 * w, None), (3 * w, RET_HEAD_DIM ** -0.5)):
                for h in range(RET_HEADS):
                    cols = slice(start + h * RET_HEAD_DIM, start + (h + 1) * RET_HEAD_DIM)
                    r = _rotary(acc[:, cols], cos2, sin2)
                    o_ref[rows, cols] = (r if scale is None else r * scale).astype(BF16)

    @pl.when(j == 2)
    def _():
        for rows, acc in project():
            o_ref[rows, :w] = acc[:, :w].astype(BF16)
            gate = acc[:, w:2 * w]
            o_ref[rows, w:2 * w] = (gate * jax.nn.sigmoid(gate)).astype(BF16)
            o_ref[rows, 2 * w:] = jax.nn.sigmoid(acc[:, 2 * w:] + bias_ref[:, 2 * w:]).astype(BF16)

    @pl.when(j == 3)
    def _():
        for rows, acc in project():
            o_ref[rows, :] = jax.nn.sigmoid(acc + bias_ref[...]).astype(BF16)


def _in_proj(x2d, g, w, b_gate, cos2, sin2, s):
    m, d = x2d.shape
    tm, tn = IN_TM, IN_TN
    bias = jnp.concatenate([jnp.zeros((IN_COLS - N_BRANCH * D_MODEL,), F32), b_gate.reshape(-1)])
    seq_blocks = s // tm
    return pl.pallas_call(
        _in_proj_kernel,
        grid=(m // tm, IN_COLS // tn),
        in_specs=[
            pl.BlockSpec((tm, d), lambda i, j: (i, 0)),
            pl.BlockSpec((1, d), lambda i, j: (0, 0)),
            pl.BlockSpec((d, tn), lambda i, j: (0, j)),
            pl.BlockSpec((1, tn), lambda i, j: (0, j)),
            pl.BlockSpec((tm, RET_HEAD_DIM), lambda i, j: (i % seq_blocks, 0)),
            pl.BlockSpec((tm, RET_HEAD_DIM), lambda i, j: (i % seq_blocks, 0)),
        ],
        out_specs=pl.BlockSpec((tm, tn), lambda i, j: (i, j)),
        out_shape=jax.ShapeDtypeStruct((m, IN_COLS), BF16),
        scratch_shapes=[pltpu.VMEM((tm, d), BF16)],
        compiler_params=_params(("parallel", "arbitrary")),
        name="in_proj",
    )(x2d, g.reshape(1, d), w, bias.reshape(1, IN_COLS), cos2, sin2)


SB_T = 256
SB_BATCH = 4
LOG2E = math.log2(math.e)
LOG2E_HI = 1.4375
LOG2E_LO = LOG2E - LOG2E_HI


def _sb_scores(qs, k_ref, ks, z_ref):
    for i, q in enumerate(qs):
        kb = k_ref[i // 2, pl.ds(ks, SB_T), :]
        z_ref[i] = lax.dot_general(q, kb, (((1,), (1,)), ((), ())), preferred_element_type=F32)


def _sb_values(v_ref, ks, a_ref, p_ref, acc_ref):
    for i in range(a_ref.shape[0]):
        vb = v_ref[i // 2, pl.ds(ks, SB_T), :]
        acc_ref[i] += p_ref[i] * jnp.dot(a_ref[i], vb, preferred_element_type=F32)


def _sb_stage1(z2, upper):
    zb = z2.astype(BF16)
    hi = jnp.maximum(zb, 0.0)
    lo = zb - hi
    ln1p = jnp.log(1.0 + jnp.exp2(lo - hi))
    log1p = ln1p * LOG2E_HI + ln1p * LOG2E_LO
    sp = hi + log1p
    suffix = jnp.dot(sp, upper, preferred_element_type=F32)
    return (lo - log1p).astype(F32), suffix, suffix[:, 0:1] + sp[:, 0:1].astype(F32)


def _sb_step(qs, k_ref, v_ref, ks, upper, z_ref, a_ref, p_ref, c_ref, acc_ref, first):
    t = SB_T
    ks_next = pl.multiple_of(jnp.maximum(ks - t, 0), t)
    stage1 = []
    for i, q in enumerate(qs):
        z2 = z_ref[i]
        _sb_scores(qs[i:i + 1], k_ref.at[i // 2:i // 2 + 1], ks_next, z_ref.at[i:i + 1])
        if not first:
            _sb_values(v_ref.at[i // 2:i // 2 + 1], pl.multiple_of(ks + t, t), a_ref.at[i:i + 1],
                       p_ref.at[i:i + 1], acc_ref.at[i:i + 1])
        stage1.append(_sb_stage1(z2, upper))
    for i, (log_beta, suffix, total) in enumerate(stage1):
        neg_carry = c_ref[i]
        a_ref[i] = jnp.exp2(log_beta - suffix).astype(BF16)
        p_ref[i] = jnp.exp2(neg_carry)
        c_ref[i] = neg_carry - total


def _sb_kernel(q_ref, k_ref, v_ref, o_ref, z_ref, a_ref, p_ref, c_ref, acc_ref):
    t = SB_T
    nb = q_ref.shape[0]
    qi = pl.program_id(1)
    lane = lax.broadcasted_iota(jnp.int32, (t, LANES), 1)
    first = lane < SB_HEAD_DIM
    qs = []
    for bi in range(nb):
        q2 = q_ref[bi]
        zero = jnp.zeros_like(q2)
        qs += [jnp.where(first, q2, zero), jnp.where(first, zero, q2)]
    row = lax.broadcasted_iota(jnp.int32, (t, t), 0)
    col = lax.broadcasted_iota(jnp.int32, (t, t), 1)
    upper = jnp.where(row > col, 1.0, 0.0).astype(BF16)
    causal = col < row

    c_ref[...] = jnp.zeros_like(c_ref)
    acc_ref[...] = jnp.zeros_like(acc_ref)
    diag = pl.multiple_of(qi * t, t)
    for i, q in enumerate(qs):
        z = lax.dot_general(q, k_ref[i // 2, pl.ds(diag, t), :], (((1,), (1,)), ((), ())),
                            preferred_element_type=F32)
        z_ref[i] = jnp.where(causal, z, -jnp.inf)
    _sb_step(qs, k_ref, v_ref, diag, upper, z_ref, a_ref, p_ref, c_ref, acc_ref, True)

    @pl.loop(0, qi)
    def _(j):
        ks = pl.multiple_of((qi - 1 - j) * t, t)
        _sb_step(qs, k_ref, v_ref, ks, upper, z_ref, a_ref, p_ref, c_ref, acc_ref, False)

    _sb_values(v_ref, 0, a_ref, p_ref, acc_ref)
    for bi in range(nb):
        o_ref[bi] = jnp.where(first, acc_ref[2 * bi], acc_ref[2 * bi + 1]).astype(o_ref.dtype)


def _sb_attention(proj, b, s):
    t, nb = SB_T, SB_BATCH
    ns = 2 * nb
    qc, kc, vc = (3 * MIX_W) // LANES, (4 * MIX_W) // LANES, (5 * MIX_W) // LANES
    pairs = MIX_W // LANES
    return pl.pallas_call(
        _sb_kernel,
        grid=(b // nb * pairs, s // t),
        in_specs=[
            pl.BlockSpec((nb, t, LANES), lambda g, i: (g // pairs, i, qc + g % pairs)),
            pl.BlockSpec((nb, s, LANES), lambda g, i: (g // pairs, 0, kc + g % pairs)),
            pl.BlockSpec((nb, s, LANES), lambda g, i: (g // pairs, 0, vc + g % pairs)),
        ],
        out_specs=pl.BlockSpec((nb, t, LANES), lambda g, i: (g // pairs, i, g % pairs)),
        out_shape=jax.ShapeDtypeStruct((b, s, MIX_W), BF16),
        scratch_shapes=[pltpu.VMEM((ns, t, t), F32), pltpu.VMEM((ns, t, t), BF16),
                        pltpu.VMEM((ns, t, 1), F32), pltpu.VMEM((ns, t, 1), F32),
                        pltpu.VMEM((ns, t, LANES), F32)],
        compiler_params=_params(("parallel", "arbitrary")),
        name="sb_attention",
    )(proj, proj, proj)


MERGE_TM = 512
RET_STEP_CHUNKS = MERGE_TM // RET_CHUNK
HALO = 8


def _retention_tile(q_ref, k_ref, v_ref, g_ref, gain_ref, dintra_ref, kdec_ref, qdec_ref,
                    cdec_ref, state_ref, out_ref):
    c, d = RET_CHUNK, RET_HEAD_DIM
    units = [(h, n) for h in range(RET_HEADS) for n in range(RET_STEP_CHUNKS)]
    blk = lambda ref, h, n: ref[0, pl.ds(n * c, c), pl.ds(h * d, d)]
    qb, scores, kv = {}, {}, {}
    for h, n in units:
        qb[h, n] = blk(q_ref, h, n)
        k = blk(k_ref, h, n)
        scores[h, n] = lax.dot_general(qb[h, n], k, (((1,), (1,)), ((), ())),
                                       preferred_element_type=F32)
        kv[h, n] = lax.dot_general((k.astype(F32) * kdec_ref[h]).astype(BF16), blk(v_ref, h, n),
                                   (((0,), (0,)), ((), ())), preferred_element_type=F32)
    inner = {}
    for h, n in units:
        p = (scores[h, n] * dintra_ref[h]).astype(BF16)
        inner[h, n] = jnp.dot(p, blk(v_ref, h, n), preferred_element_type=F32)
    cross = {}
    for h in range(RET_HEADS):
        state = state_ref[h]
        for n in range(RET_STEP_CHUNKS):
            cross[h, n] = jnp.dot(qb[h, n], state.astype(BF16), preferred_element_type=F32)
            state = cdec_ref[h] * state + kv[h, n]
        state_ref[h] = state
    for h, n in units:
        o = inner[h, n] + cross[h, n] * qdec_ref[h]
        mu = jnp.mean(o, axis=-1, keepdims=True)
        oc = o - mu
        var = jnp.mean(oc * oc, axis=-1, keepdims=True)
        on = oc * lax.rsqrt(var + EPS) * gain_ref[:, pl.ds(h * d, d)]
        out_ref[pl.ds(n * c, c), pl.ds(h * d, d)] = (
            blk(g_ref, h, n).astype(F32) * on).astype(out_ref.dtype)


def _rotary_tables(s):
    pos = jnp.arange(s, dtype=F32)
    inv_freq = ROPE_BASE ** (-jnp.arange(0, RET_HEAD_DIM, 2, dtype=F32) / RET_HEAD_DIM)
    ang = pos[:, None] * inv_freq[None, :]
    cos, sin = jnp.cos(ang), jnp.sin(ang)
    return jnp.concatenate([cos, cos], axis=-1), jnp.concatenate([-sin, sin], axis=-1)


def _retention_tables():
    log_gamma = jnp.log1p(-jnp.exp2(-5.0 - jnp.arange(RET_HEADS, dtype=F32)))
    idx = jnp.arange(RET_CHUNK, dtype=F32)
    rel = idx[:, None] - idx[None, :]
    dintra = jnp.where(rel >= 0, jnp.exp(jnp.maximum(rel, 0.0)[None] * log_gamma[:, None, None]), 0.0)
    kdec = jnp.exp((RET_CHUNK - 1 - idx)[None] * log_gamma[:, None])
    qdec = jnp.exp((idx + 1)[None] * log_gamma[:, None])
    cdec = jnp.exp(RET_CHUNK * log_gamma)
    bc = lambda a: jnp.broadcast_to(a[:, :, None], (RET_HEADS, RET_CHUNK, RET_HEAD_DIM))
    cdec = jnp.broadcast_to(cdec[:, None, None], (RET_HEADS, 1, RET_HEAD_DIM))
    return dintra, bc(kdec), bc(qdec), cdec


def _merge_kernel(x_ref, cb_ref, cc_ref, ch_ref, ccp_ref, chp_ref, gate0_ref, gate1_ref,
                  gate2_ref, sb_ref, rq_ref, rk_ref, rv_ref, rg_ref, convw_ref, convb_ref, wbr_ref,
                  wo_ref, gain_ref, dintra_ref, kdec_ref, qdec_ref, cdec_ref, o_ref, state_ref,
                  ret_ref):
    tm = MERGE_TM

    @pl.when(pl.program_id(1) == 0)
    def _():
        state_ref[...] = jnp.zeros_like(state_ref)

    u = cc_ref[0].astype(F32) * ch_ref[0].astype(F32)
    prev = ccp_ref[0].astype(F32) * chp_ref[0].astype(F32)
    prev = jnp.where(pl.program_id(1) == 0, 0.0, prev)
    row = lax.broadcasted_iota(jnp.int32, (tm, MIX_W), 0)
    u1 = jnp.where(row == 0, prev[HALO - 1:HALO, :], pltpu.roll(u, 1, 0))
    u2 = jnp.where(row == 0, prev[HALO - 2:HALO - 1, :],
                   jnp.where(row == 1, prev[HALO - 1:HALO, :], pltpu.roll(u, 2, 0)))
    conv = convb_ref[...] + u2 * convw_ref[0:1, :] + u1 * convw_ref[1:2, :] + u * convw_ref[2:3, :]
    conv_out = (cb_ref[0].astype(F32) * conv).astype(BF16)
    merged = gate0_ref[0].astype(F32) * jnp.dot(conv_out, wbr_ref[0], preferred_element_type=F32)
    merged = merged + gate1_ref[0].astype(F32) * jnp.dot(sb_ref[0], wbr_ref[1],
                                                         preferred_element_type=F32)
    _retention_tile(rq_ref, rk_ref, rv_ref, rg_ref, gain_ref, dintra_ref, kdec_ref, qdec_ref,
                    cdec_ref, state_ref, ret_ref)
    merged = merged + gate2_ref[0].astype(F32) * jnp.dot(ret_ref[...], wbr_ref[2],
                                                         preferred_element_type=F32)
    o_ref[0] = x_ref[0] + jnp.dot(merged.astype(BF16), wo_ref[...], preferred_element_type=F32)


def _merge(x, proj, sb_out, ret_gain, conv_w, conv_b, w_branch, w_o):
    b, s, d = x.shape
    tm = MERGE_TM
    w = MIX_W
    c, hd = RET_CHUNK, RET_HEAD_DIM
    gate_blk = (10 * MIX_W) // d
    ret_blk = 6
    dintra, kdec, qdec, cdec = _retention_tables()
    tok = lambda width, cblk: pl.BlockSpec((1, tm, width), lambda bi, i: (bi, i, cblk))
    halo = lambda cblk: pl.BlockSpec(
        (1, HALO, w), lambda bi, i: (bi, jnp.maximum(i * (tm // HALO) - 1, 0), cblk))
    full = lambda shape: pl.BlockSpec(shape, lambda bi, i: (0,) * len(shape))
    return pl.pallas_call(
        _merge_kernel,
        grid=(b, s // tm),
        in_specs=[tok(d, 0), tok(w, 0), tok(w, 1), tok(w, 2), halo(1), halo(2),
                  tok(d, gate_blk), tok(d, gate_blk + 1), tok(d, gate_blk + 2), tok(w, 0),
                  tok(w, ret_blk), tok(w, ret_blk + 1), tok(w, ret_blk + 2), tok(w, ret_blk + 3),
                  full((CONV_K, w)), full((1, w)), full((N_BRANCH, w, d)), full((d, d)),
                  full((1, w)), full((RET_HEADS, c, c)), full((RET_HEADS, c, hd)),
                  full((RET_HEADS, c, hd)), full((RET_HEADS, 1, hd))],
        out_specs=tok(d, 0),
        out_shape=jax.ShapeDtypeStruct((b, s, d), F32),
        scratch_shapes=[pltpu.VMEM((RET_HEADS, hd, hd), F32), pltpu.VMEM((tm, w), BF16)],
        compiler_params=_params(("parallel", "arbitrary")),
        name="merge",
    )(x, proj, proj, proj, proj, proj, proj, proj, proj, sb_out, proj, proj, proj, proj, conv_w,
      conv_b.reshape(1, w), w_branch, w_o, ret_gain.reshape(1, w), dintra, kdec, qdec, cdec)


XA_TM = 1024


def _xattn_kernel(x_ref, g_ref, wq_ref, kv_ref, wo_ref, o_ref):
    x = x_ref[0]
    h = _rmsnorm_f32(x, g_ref[...]).astype(BF16)
    q = jnp.dot(h, wq_ref[...], preferred_element_type=F32)
    q = (q * (XA_HEAD_DIM ** -0.5)).astype(BF16)
    heads = [slice(hd * XA_HEAD_DIM, (hd + 1) * XA_HEAD_DIM) for hd in range(XA_HEADS)]
    scores = [lax.dot_general(q[:, cols], kv_ref[0, :, cols], (((1,), (1,)), ((), ())),
                              preferred_element_type=F32) for cols in heads]
    outs = []
    for cols, sc in zip(heads, scores):
        e = jnp.exp(sc - jnp.max(sc, axis=-1, keepdims=True))
        p = e / jnp.sum(e, axis=-1, keepdims=True)
        vh = kv_ref[0, :, D_MODEL + cols.start:D_MODEL + cols.stop]
        outs.append(jnp.dot(p.astype(BF16), vh, preferred_element_type=F32).astype(BF16))
    o = jnp.concatenate(outs, axis=-1)
    o_ref[0] = x + jnp.dot(o, wo_ref[...], preferred_element_type=F32)


def _xattn(x, g, w_q, kv, w_o):
    b, s, d = x.shape
    tm = XA_TM
    full = lambda shape: pl.BlockSpec(shape, lambda bi, i: (0,) * len(shape))
    return pl.pallas_call(
        _xattn_kernel,
        grid=(b, s // tm),
        in_specs=[pl.BlockSpec((1, tm, d), lambda bi, i: (bi, i, 0)), full((1, d)), full((d, d)),
                  pl.BlockSpec((1, MEM_LEN, 2 * d), lambda bi, i: (bi, 0, 0)), full((d, d))],
        out_specs=pl.BlockSpec((1, tm, d), lambda bi, i: (bi, i, 0)),
        out_shape=jax.ShapeDtypeStruct((b, s, d), F32),
        compiler_params=_params(("parallel", "parallel")),
        name="xattn",
    )(x, g.reshape(1, d), w_q, kv, w_o)


MLP_TM = 1024
MLP_TF = 1024
MLP_GROUPS = 2


def _mlp_kernel(x_ref, g_ref, wu_ref, wd_ref, fg_ref, o_ref, h_ref, acc_ref, *, final_norm):
    j = pl.program_id(1)

    rows = [pl.ds(r * (MLP_TM // MLP_GROUPS), MLP_TM // MLP_GROUPS) for r in range(MLP_GROUPS)]

    def step(first):
        if first:
            for r in rows:
                h_ref[r, :] = _rmsnorm_f32(x_ref[r, :], g_ref[...]).astype(BF16)
        ups = [jnp.dot(h_ref[r, :], wu_ref[...], preferred_element_type=F32) for r in rows]
        acts = [jnp.square(jnp.maximum(up, 0.0)).astype(BF16) for up in ups]
        for r, act in zip(rows, acts):
            down = jnp.dot(act, wd_ref[...], preferred_element_type=F32)
            acc_ref[r, :] = (x_ref[r, :] if first else acc_ref[r, :]) + down

    pl.when(j == 0)(functools.partial(step, True))
    pl.when(j > 0)(functools.partial(step, False))

    @pl.when(j == pl.num_programs(1) - 1)
    def _():
        y = acc_ref[...]
        o_ref[...] = _rmsnorm_f32(y, fg_ref[...]) if final_norm else y


def _mlp(x2d, g, w_up, w_down, final_g, final_norm):
    m, d = x2d.shape
    f = w_up.shape[1]
    tm, tf = MLP_TM, MLP_TF
    return pl.pallas_call(
        functools.partial(_mlp_kernel, final_norm=final_norm),
        grid=(m // tm, f // tf),
        in_specs=[pl.BlockSpec((tm, d), lambda i, j: (i, 0)),
                  pl.BlockSpec((1, d), lambda i, j: (0, 0)),
                  pl.BlockSpec((d, tf), lambda i, j: (0, j)),
                  pl.BlockSpec((tf, d), lambda i, j: (j, 0)),
                  pl.BlockSpec((1, d), lambda i, j: (0, 0))],
        out_specs=pl.BlockSpec((tm, d), lambda i, j: (i, 0)),
        out_shape=jax.ShapeDtypeStruct((m, d), F32),
        scratch_shapes=[pltpu.VMEM((tm, d), BF16), pltpu.VMEM((tm, d), F32)],
        compiler_params=_params(("parallel", "arbitrary")),
        name="mlp",
    )(x2d, g.reshape(1, d), w_up, w_down, final_g.reshape(1, d))


def kernel(x, mem, norm_mix_g, w_in, b_gate, conv_w, conv_b, ret_norm_g, w_branch, w_o,
           norm_xa_g, norm_mem_g, w_xq, w_xkv, w_xo, norm_mlp_g, w_up, w_down, final_g):
    b, s, d = x.shape
    depth = w_in.shape[0]
    assert d == D_MODEL and b % SB_BATCH == 0 and (b * s) % MLP_TM == 0
    assert all(s % tile == 0 for tile in (SB_T, MERGE_TM, XA_TM, IN_TM))
    mem2d = mem.reshape(b * MEM_LEN, d)
    col_scale = jnp.ones((IN_COLS,), F32).at[3 * MIX_W:4 * MIX_W].set(SB_HEAD_DIM ** -0.5 * LOG2E)
    cos2, sin2 = _rotary_tables(s)
    for l in range(depth):
        w_in_l = (w_in[l] * col_scale).astype(BF16)
        proj = _in_proj(x.reshape(b * s, d), norm_mix_g[l], w_in_l, b_gate[l], cos2, sin2, s)
        proj = proj.reshape(b, s, IN_COLS)
        sb_out = _sb_attention(proj, b, s)
        x = _merge(x, proj, sb_out, ret_norm_g[l], conv_w[l], conv_b[l], w_branch[l].astype(BF16),
                   w_o[l].astype(BF16))
        kv = _norm_matmul(mem2d, norm_mem_g[l], w_xkv[l].astype(BF16), b * MEM_LEN, 1024)
        x = _xattn(x, norm_xa_g[l], w_xq[l].astype(BF16), kv.reshape(b, MEM_LEN, 2 * d),
                   w_xo[l].astype(BF16))
        x = _mlp(x.reshape(b * s, d), norm_mlp_g[l], w_up[l].astype(BF16), w_down[l].astype(BF16),
                 final_g, l == depth - 1).reshape(b, s, d)
    return x
```
